```python
import jax, jax.numpy as jnp
from jax import lax
import numpy as np

D_MODEL = 1024
BATCH = 2
SEQ = 16384
DEPTH = 2

HEAD_DIM = 64
Q_BLOCK = 128
EPS = 1e-6
NEG_INF = -1e30
A_HEADS = 8
A_PATTERNS = ((128, 1), (512, 4), (2048, 16))
B_HEADS = 4
B_DK = 64
B_DV = 128
B_GATE_RANK = 16
B_GATE_TAU = 16.0
B_CHUNK = 64
C_HEADS = 8
C_KV_HEADS = 2
C_WINDOW = 128
D_HEADS = 8
D_Q_RANK = 384
D_KV_RANK = 256
D_NOPE = 64
D_ROPE = 32
D_V = 64
ROPE_BASE = 10000.0
D_FF = -(-8 * D_MODEL // (3 * 256)) * 256

L0_IN_SIZES = (A_HEADS * HEAD_DIM, A_HEADS * HEAD_DIM, A_HEADS * HEAD_DIM,
               B_HEADS * B_DK, B_HEADS * B_DK, B_HEADS * B_DV, B_HEADS * B_DV, B_GATE_RANK)
L0_IN = sum(L0_IN_SIZES)
L0_MIX = A_HEADS * HEAD_DIM + B_HEADS * B_DV
L1_IN_SIZES = (C_HEADS * HEAD_DIM, C_KV_HEADS * HEAD_DIM, C_KV_HEADS * HEAD_DIM,
               D_Q_RANK, D_KV_RANK, D_ROPE)
L1_IN = sum(L1_IN_SIZES)
L1_MIX = C_HEADS * HEAD_DIM + D_HEADS * D_V
N_EVEN = (DEPTH + 1) // 2
N_ODD = DEPTH // 2

kernel_name = "hybrid_dilated_gla_swasink_mla_block"


def rmsnorm(x, g):
    xf = x.astype(jnp.float32)
    y = xf * lax.rsqrt(jnp.mean(xf * xf, axis=-1, keepdims=True) + EPS)
    return (y * g.astype(jnp.float32)).astype(x.dtype)


def alibi_slopes(n):
    return jnp.asarray(np.array([2.0 ** (-8.0 * (i + 1) / n) for i in range(n)], dtype=np.float32))


def split_cols(t, sizes):
    return jnp.split(t, [int(c) for c in np.cumsum(sizes)[:-1]], axis=-1)


def with_prev_block(t, blk):
    lead = t.shape[:-2]
    L, dh = t.shape[-2:]
    tb = t.reshape(*lead, L // blk, blk, dh)
    prev = jnp.concatenate([jnp.zeros_like(tb[..., :1, :, :]), tb[..., :-1, :, :]], axis=-3)
    return jnp.concatenate([prev, tb], axis=-2)


def band_geometry(nb, blk, max_rel):
    i = jnp.arange(blk)[:, None]
    j = jnp.arange(2 * blk)[None, :]
    rel = i + blk - j
    key_pos = jnp.arange(nb)[:, None, None] * blk + j - blk
    valid = (rel >= 0) & (rel <= max_rel) & (key_pos >= 0)
    return rel.astype(jnp.float32), valid


def rope(t, pos):
    half = t.shape[-1] // 2
    freqs = ROPE_BASE ** (-jnp.arange(half, dtype=jnp.float32) / half)
    ang = pos.astype(jnp.float32)[:, None] * freqs[None, :]
    cos, sin = jnp.cos(ang)[:, None, :], jnp.sin(ang)[:, None, :]
    tf = t.astype(jnp.float32)
    t1, t2 = tf[..., :half], tf[..., half:]
    return jnp.concatenate([t1 * cos - t2 * sin, t2 * cos + t1 * sin], axis=-1).astype(t.dtype)


def dilated_branch(q, k, v, slopes, window, dilation):
    B_, H, S, dh = q.shape
    L = S // dilation
    Lp = -(-L // Q_BLOCK) * Q_BLOCK
    nb = Lp // Q_BLOCK

    def by_residue(t):
        t = t.reshape(B_, H, L, dilation, dh).transpose(0, 1, 3, 2, 4)
        return jnp.pad(t, ((0, 0), (0, 0), (0, 0), (0, Lp - L), (0, 0)))

    qb = by_residue(q).reshape(B_, H, dilation, nb, Q_BLOCK, dh)
    kb = with_prev_block(by_residue(k), Q_BLOCK)
    vb = with_prev_block(by_residue(v), Q_BLOCK).astype(jnp.float32)
    rel, valid = band_geometry(nb, Q_BLOCK, window // dilation)
    s = jnp.einsum('bhrnid,bhrnjd->bhrnij', qb, kb, preferred_element_type=jnp.float32) * dh ** -0.5
    s = s - slopes[:, None, None, None, None] * (rel * dilation)
    s = jnp.where(valid, s, NEG_INF)
    m = s.max(-1)
    p = jnp.exp(s - m[..., None])
    den = p.sum(-1)
    o = jnp.einsum('bhrnij,bhrnjd->bhrnid', p, vb) / den[..., None]

    def back(t):
        t = t.reshape(B_, H, dilation, Lp, *t.shape[5:])[:, :, :, :L]
        t = jnp.moveaxis(t, 2, 3)
        return t.reshape(B_, H, S, *t.shape[4:])

    return back(o), back(m), back(den)


def dilated_attention(q, k, v):
    slopes = alibi_slopes(A_HEADS)
    outs, maxes, dens = zip(*[dilated_branch(q, k, v, slopes, w, d) for (w, d) in A_PATTERNS])
    m_all = jnp.stack(maxes)
    wts = jnp.stack(dens) * jnp.exp(m_all - m_all.max(0))
    return jnp.einsum('pbhs,pbhsd->bhsd', wts, jnp.stack(outs)) / wts.sum(0)[..., None]


def gla(q, k, v, r, g_low, w_gate_up, b_gate, norm_g):
    B_, S, _ = q.shape
    C = B_CHUNK
    nc = S // C
    log_a = jax.nn.log_sigmoid((g_low @ w_gate_up + b_gate).astype(jnp.float32)) / B_GATE_TAU

    def chunks(t, dh):
        return t.reshape(B_, nc, C, B_HEADS, dh).transpose(0, 3, 1, 2, 4).astype(jnp.float32)

    qc = chunks(q, B_DK) * B_DK ** -0.5
    kc = chunks(k, B_DK)
    vc = chunks(v, B_DV)
    bcum = jnp.cumsum(chunks(log_a, B_DK), axis=-2)
    b_last = bcum[..., -1:, :]
    b_mid = bcum[..., C // 2 - 1:C // 2, :]
    att = jnp.einsum('bhncd,bhnsd->bhncs', qc * jnp.exp(bcum - b_mid), kc * jnp.exp(b_mid - bcum))
    att = jnp.where(jnp.tril(jnp.ones((C, C), dtype=bool)), att, 0.0)
    o_intra = jnp.einsum('bhncs,bhnse->bhnce', att, vc)
    dS = jnp.einsum('bhnsd,bhnse->bhnde', kc * jnp.exp(b_last - bcum), vc)
    decay = jnp.exp(b_last[..., 0, :])

    def step(state, inp):
        dS_n, dec_n = inp
        return dec_n[..., None] * state + dS_n, state

    init = jnp.zeros((B_, B_HEADS, B_DK, B_DV), jnp.float32)
    _, s_prev = lax.scan(step, init, (jnp.moveaxis(dS, 2, 0), jnp.moveaxis(decay, 2, 0)))
    s_prev = jnp.moveaxis(s_prev, 0, 2)
    o_inter = jnp.einsum('bhncd,bhnde->bhnce', qc * jnp.exp(bcum), s_prev)
    o = (o_intra + o_inter).transpose(0, 2, 3, 1, 4).reshape(B_, S, B_HEADS, B_DV)
    o = rmsnorm(o, norm_g)
    o = o * jax.nn.silu(r.astype(jnp.float32)).reshape(B_, S, B_HEADS, B_DV)
    return o.reshape(B_, S, B_HEADS * B_DV)


def swa_sink_attention(q, k, v, sinks):
    B_, S, _ = q.shape
    G = C_HEADS // C_KV_HEADS
    nb = S // Q_BLOCK
    qb = q.reshape(B_, nb, Q_BLOCK, C_KV_HEADS, G, HEAD_DIM).transpose(0, 3, 4, 1, 2, 5)
    kt = k.reshape(B_, S, C_KV_HEADS, HEAD_DIM).transpose(0, 2, 1, 3)
    vt = v.reshape(B_, S, C_KV_HEADS, HEAD_DIM).transpose(0, 2, 1, 3)
    kb = with_prev_block(kt, Q_BLOCK)
    vb = with_prev_block(vt, Q_BLOCK).astype(jnp.float32)
    rel, valid = band_geometry(nb, Q_BLOCK, C_WINDOW - 1)
    slopes = alibi_slopes(C_HEADS).reshape(C_KV_HEADS, G)[:, :, None, None, None]
    s = jnp.einsum('bkgnid,bknjd->bkgnij', qb, kb, preferred_element_type=jnp.float32) * HEAD_DIM ** -0.5
    s = jnp.where(valid, s - slopes * rel, NEG_INF)
    sink = sinks.astype(jnp.float32).reshape(C_KV_HEADS, G)[:, :, None, None, None]
    m = jnp.maximum(s.max(-1, keepdims=True), sink)
    p = jnp.exp(s - m)
    den = p.sum(-1, keepdims=True) + jnp.exp(sink - m)
    o = jnp.einsum('bkgnij,bknjd->bkgnid', p / den, vb)
    return o.transpose(0, 3, 4, 1, 2, 5).reshape(B_, S, C_HEADS * HEAD_DIM)


def mla(c_q, c_kv, k_rope, q_norm, w_uq, kv_norm, w_ukv):
    B_, S, _ = c_q.shape
    pos = jnp.arange(S)
    q = (rmsnorm(c_q, q_norm) @ w_uq).reshape(B_, S, D_HEADS, D_NOPE + D_ROPE)
    kv = (rmsnorm(c_kv, kv_norm) @ w_ukv).reshape(B_, S, D_HEADS, D_NOPE + D_V)
    q_nope, q_pe = q[..., :D_NOPE], rope(q[..., D_NOPE:], pos)
    k_nope, v = kv[..., :D_NOPE], kv[..., D_NOPE:].astype(jnp.float32)
    k_pe = rope(k_rope[:, :, None, :], pos)[:, :, 0, :]
    scale = (D_NOPE + D_ROPE) ** -0.5
    nb = S // Q_BLOCK
    qn_b = q_nope.reshape(B_, nb, Q_BLOCK, D_HEADS, D_NOPE).transpose(1, 0, 2, 3, 4)
    qp_b = q_pe.reshape(B_, nb, Q_BLOCK, D_HEADS, D_ROPE).transpose(1, 0, 2, 3, 4)
    kpos = jnp.arange(S)

    def block(args):
        qn, qp, n = args
        s = (jnp.einsum('bihd,bjhd->bhij', qn, k_nope, preferred_element_type=jnp.float32)
             + jnp.einsum('bihd,bjd->bhij', qp, k_pe, preferred_element_type=jnp.float32)) * scale
        qpos = n * Q_BLOCK + jnp.arange(Q_BLOCK)
        s = jnp.where(qpos[:, None] >= kpos[None, :], s, NEG_INF)
        return jnp.einsum('bhij,bjhd->bihd', jax.nn.softmax(s, axis=-1), v)

    o = lax.map(block, (qn_b, qp_b, jnp.arange(nb)))
    return o.transpose(1, 0, 2, 3, 4).reshape(B_, S, D_HEADS * D_V)


def mix_even(h, w_in, w_out, gla_w_gate_up, gla_b_gate, gla_norm):
    B_, S, _ = h.shape
    qa, ka, va, qb, kb, vb, rb, gb = split_cols(h @ w_in, L0_IN_SIZES)
    heads = lambda t: t.reshape(B_, S, A_HEADS, HEAD_DIM).transpose(0, 2, 1, 3)
    o_a = dilated_attention(heads(qa), heads(ka), heads(va))
    o_a = o_a.transpose(0, 2, 1, 3).reshape(B_, S, A_HEADS * HEAD_DIM)
    o_b = gla(qb, kb, vb, rb, gb, gla_w_gate_up, gla_b_gate, gla_norm)
    return jnp.concatenate([o_a, o_b], axis=-1).astype(h.dtype) @ w_out


def mix_odd(h, w_in, w_out, sinks, q_norm, w_uq, kv_norm, w_ukv):
    qc, kc, vc, c_q, c_kv, k_rope = split_cols(h @ w_in, L1_IN_SIZES)
    o_c = swa_sink_attention(qc, kc, vc, sinks)
    o_d = mla(c_q, c_kv, k_rope, q_norm, w_uq, kv_norm, w_ukv)
    return jnp.concatenate([o_c, o_d], axis=-1).astype(h.dtype) @ w_out


def swiglu(h, w_gate, w_up, w_down):
    return (jax.nn.silu(h @ w_gate) * (h @ w_up)) @ w_down


def setup_inputs(seed: int = 0) -> dict:
    key = jax.random.key(seed)
    ks = jax.random.split(key, 24)
    nrm = lambda k, shape, fan_in: jax.random.normal(k, shape, jnp.float32) * fan_in ** -0.5
    gain = lambda k, shape: 1.0 + 0.02 * jax.random.normal(k, shape, jnp.float32)
    return {
        "x": jax.random.normal(ks[0], (BATCH, SEQ, D_MODEL), jnp.float32),
        "norm_mix_pre": gain(ks[1], (DEPTH, D_MODEL)),
        "norm_mix_post": gain(ks[2], (DEPTH, D_MODEL)),
        "norm_ffn_pre": gain(ks[3], (DEPTH, D_MODEL)),
        "norm_ffn_post": gain(ks[4], (DEPTH, D_MODEL)),
        "ffn_w_gate": nrm(ks[5], (DEPTH, D_MODEL, D_FF), D_MODEL),
        "ffn_w_up": nrm(ks[6], (DEPTH, D_MODEL, D_FF), D_MODEL),
        "ffn_w_down": nrm(ks[7], (DEPTH, D_FF, D_MODEL), D_FF),
        "ab_w_in": nrm(ks[8], (N_EVEN, D_MODEL, L0_IN), D_MODEL),
        "ab_w_out": nrm(ks[9], (N_EVEN, L0_MIX, D_MODEL), L0_MIX),
        "gla_w_gate_up": nrm(ks[10], (N_EVEN, B_GATE_RANK, B_HEADS * B_DK), B_GATE_RANK),
        "gla_b_gate": 0.1 * jax.random.normal(ks[11], (N_EVEN, B_HEADS * B_DK), jnp.float32),
        "gla_norm": gain(ks[12], (N_EVEN, B_DV)),
        "cd_w_in": nrm(ks[13], (N_ODD, D_MODEL, L1_IN), D_MODEL),
        "cd_w_out": nrm(ks[14], (N_ODD, L1_MIX, D_MODEL), L1_MIX),
        "swa_sinks": jax.random.normal(ks[15], (N_ODD, C_HEADS), jnp.float32),
        "mla_q_norm": gain(ks[16], (N_ODD, D_Q_RANK)),
        "mla_w_uq": nrm(ks[17], (N_ODD, D_Q_RANK, D_HEADS * (D_NOPE + D_ROPE)), D_Q_RANK),
        "mla_kv_norm": gain(ks[18], (N_ODD, D_KV_RANK)),
        "mla_w_ukv": nrm(ks[19], (N_ODD, D_KV_RANK, D_HEADS * (D_NOPE + D_V)), D_KV_RANK),
    }


def reference(x, norm_mix_pre, norm_mix_post, norm_ffn_pre, norm_ffn_post,
              ffn_w_gate, ffn_w_up, ffn_w_down,
              ab_w_in, ab_w_out, gla_w_gate_up, gla_b_gate, gla_norm,
              cd_w_in, cd_w_out, swa_sinks, mla_q_norm, mla_w_uq, mla_kv_norm, mla_w_ukv):
    for layer in range(DEPTH):
        i = layer // 2
        h = rmsnorm(x, norm_mix_pre[layer])
        if layer % 2 == 0:
            y = mix_even(h, ab_w_in[i], ab_w_out[i], gla_w_gate_up[i], gla_b_gate[i], gla_norm[i])
        else:
            y = mix_odd(h, cd_w_in[i], cd_w_out[i], swa_sinks[i], mla_q_norm[i], mla_w_uq[i],
                        mla_kv_norm[i], mla_w_ukv[i])
        x = x + rmsnorm(y, norm_mix_post[layer]).astype(x.dtype)
        h = rmsnorm(x, norm_ffn_pre[layer])
        f = swiglu(h, ffn_w_gate[layer], ffn_w_up[layer], ffn_w_down[layer])
        x = x + rmsnorm(f, norm_ffn_post[layer]).astype(x.dtype)
    return x
```

```python
import functools

import numpy as np
import jax
import jax.numpy as jnp
from jax import lax
from jax.experimental import pallas as pl
from jax.experimental.pallas import tpu as pltpu

F32 = jnp.float32
BF16 = jnp.bfloat16

LANES = 128
V7X_VMEM_BYTES = 64 * 1024 * 1024
VMEM_LIMIT = 56 * 1024 * 1024

D_MODEL = 1024
HEAD_DIM = 64
Q_BLOCK = 128
EPS = 1e-6
NEG_INF = -1e30
A_HEADS = 8
A_PATTERNS = ((128, 1), (512, 4), (2048, 16))
B_HEADS = 4
B_DK = 64
B_DV = 128
B_GATE_RANK = 16
B_GATE_TAU = 16.0
B_CHUNK = 64
C_HEADS = 8
C_KV_HEADS = 2
C_WINDOW = 128
D_HEADS = 8
D_Q_RANK = 384
D_KV_RANK = 256
D_NOPE = 64
D_ROPE = 32
D_V = 64
ROPE_BASE = 10000.0
D_FF = 2816


def _alibi_slopes(n):
    return [float(np.float32(2.0 ** (-8.0 * (i + 1) / n))) for i in range(n)]


def _params(semantics):
    return pltpu.CompilerParams(dimension_semantics=semantics, vmem_limit_bytes=VMEM_LIMIT)


def _rms(x, g):
    return x * lax.rsqrt(jnp.mean(x * x, axis=-1, keepdims=True) + EPS) * g


def _dot(a, b):
    return jnp.dot(a, b, preferred_element_type=F32)


def _dot_nt(a, b):
    return lax.dot_general(a, b, (((1,), (1,)), ((), ())), preferred_element_type=F32)


def _dot_tn(a, b):
    return lax.dot_general(a, b, (((0,), (0,)), ((), ())), preferred_element_type=F32)


def _norm_proj_kernel(x_ref, g_ref, w_ref, *out_refs, splits):
    h = _rms(x_ref[...], g_ref[...]).astype(BF16)
    for o_ref, (start, size, scale) in zip(out_refs, splits):
        y = _dot(h, w_ref[:, start:start + size])
        if scale != 1.0:
            y = y * scale
        o_ref[...] = y.astype(o_ref.dtype)


def _norm_proj(x2d, g, w, splits, dtypes, tm):
    n, d = x2d.shape
    wn = w.shape[1]
    out_shape = [jax.ShapeDtypeStruct((n, size), dt) for (_, size, _), dt in zip(splits, dtypes)]
    out_specs = [pl.BlockSpec((tm, size), lambda i: (i, 0)) for (_, size, _) in splits]
    return pl.pallas_call(
        functools.partial(_norm_proj_kernel, splits=splits),
        grid=(n // tm,),
        in_specs=[pl.BlockSpec((tm, d), lambda i: (i, 0)),
                  pl.BlockSpec((1, d), lambda i: (0, 0)),
                  pl.BlockSpec((d, wn), lambda i: (0, 0))],
        out_specs=out_specs,
        out_shape=out_shape,
        compiler_params=_params(("parallel",)),
        name="norm_proj",
    )(x2d, g.reshape(1, d), w)


def _band_attn_kernel(*refs, n_pairs, kv_of_pair, slopes, dilation, max_rel, tq,
                      with_sinks, with_stats):
    if with_sinks:
        sink_ref, refs = refs[0], refs[1:]
    q_ref, kc_ref, kp_ref, vc_ref, vp_ref = refs[:5]
    if with_stats:
        o_ref, m_ref, l_ref, kcat, vcat = refs[5:]
    else:
        o_ref, kcat, vcat = refs[5:]

    kcat[0:Q_BLOCK, :] = kp_ref[0]
    kcat[Q_BLOCK:, :] = kc_ref[0]
    vcat[0:Q_BLOCK, :] = vp_ref[0]
    vcat[Q_BLOCK:, :] = vc_ref[0]

    first_tile = pl.program_id(2) == 0
    row = lax.broadcasted_iota(jnp.int32, (Q_BLOCK, 2 * Q_BLOCK), 0)
    col = lax.broadcasted_iota(jnp.int32, (Q_BLOCK, 2 * Q_BLOCK), 1)
    rel_i = row + Q_BLOCK - col
    band = (rel_i >= 0) & (rel_i <= max_rel)
    dist = (rel_i * dilation).astype(F32)
    lane = lax.broadcasted_iota(jnp.int32, (Q_BLOCK, LANES), 1)
    low = lane < HEAD_DIM

    def q_block(t, carry):
        r0 = pl.multiple_of(t * Q_BLOCK, Q_BLOCK)
        valid = band & ((col >= Q_BLOCK) | jnp.logical_not(first_tile & (t == 0)))
        for hp in range(n_pairs):
            qblk = q_ref[0, pl.ds(r0, Q_BLOCK), hp * LANES:(hp + 1) * LANES]
            kv = kv_of_pair[hp]
            kblk = kcat[pl.ds(r0, 2 * Q_BLOCK), kv * LANES:(kv + 1) * LANES]
            vblk = vcat[pl.ds(r0, 2 * Q_BLOCK), kv * LANES:(kv + 1) * LANES]
            outs, ms, ls = [], [], []
            for e in range(2):
                h = 2 * hp + e
                qm = jnp.where(low if e == 0 else jnp.logical_not(low), qblk, jnp.zeros_like(qblk))
                s = _dot_nt(qm, kblk)
                s = jnp.where(valid, s - slopes[h] * dist, NEG_INF)
                m = jnp.max(s, axis=-1, keepdims=True)
                if with_sinks:
                    sink = sink_ref[h]
                    m = jnp.maximum(m, sink)
                p = jnp.exp(s - m)
                den = jnp.sum(p, axis=-1, keepdims=True)
                if with_sinks:
                    den = den + jnp.exp(sink - m)
                o = _dot(p.astype(BF16), vblk) / den
                outs.append(o)
                ms.append(m)
                ls.append(den)
            cols = slice(hp * LANES, (hp + 1) * LANES)
            o_ref[0, pl.ds(r0, Q_BLOCK), cols] = jnp.where(low, outs[0], outs[1]).astype(o_ref.dtype)
            if with_stats:
                m_ref[0, pl.ds(r0, Q_BLOCK), cols] = jnp.where(low, ms[0], ms[1])
                l_ref[0, pl.ds(r0, Q_BLOCK), cols] = jnp.where(low, ls[0], ls[1])
        return carry

    lax.fori_loop(0, tq // Q_BLOCK, q_block, 0)


def _band_attn(q, k, v, *, batch, dilation, max_rel, slopes, kv_of_pair, sinks=None,
               with_stats=False, out_dtype=F32):
    n, cq = q.shape
    ck = k.shape[1]
    seq = n // batch
    d = dilation
    length = seq // d
    tq = min(1024, length)
    n_pairs = cq // LANES
    nb_per_tile = tq // Q_BLOCK
    qv = q.reshape(batch, length, d * cq)
    kv = k.reshape(batch, length, d * ck)
    vv = v.reshape(batch, length, d * ck)
    cur = lambda b, r, i: (b, i, r)
    prev = lambda b, r, i: (b, jnp.maximum(i * nb_per_tile - 1, 0), r)
    in_specs = [pl.BlockSpec((1, tq, cq), cur),
                pl.BlockSpec((1, tq, ck), cur),
                pl.BlockSpec((1, Q_BLOCK, ck), prev),
                pl.BlockSpec((1, tq, ck), cur),
                pl.BlockSpec((1, Q_BLOCK, ck), prev)]
    args = [qv, kv, kv, vv, vv]
    if sinks is not None:
        in_specs = [pl.BlockSpec(memory_space=pltpu.SMEM)] + in_specs
        args = [sinks] + args
    o_spec = pl.BlockSpec((1, tq, cq), cur)
    out_shape = [jax.ShapeDtypeStruct((batch, length, d * cq), out_dtype)]
    out_specs = [o_spec]
    if with_stats:
        out_shape += [jax.ShapeDtypeStruct((batch, length, d * cq), F32)] * 2
        out_specs += [o_spec, o_spec]
    outs = pl.pallas_call(
        functools.partial(_band_attn_kernel, n_pairs=n_pairs, kv_of_pair=kv_of_pair,
                          slopes=slopes, dilation=d, max_rel=max_rel, tq=tq,
                          with_sinks=sinks is not None, with_stats=with_stats),
        grid=(batch, d, length // tq),
        in_specs=in_specs,
        out_specs=out_specs,
        out_shape=out_shape,
        scratch_shapes=[pltpu.VMEM((tq + Q_BLOCK, ck), BF16),
                        pltpu.VMEM((tq + Q_BLOCK, ck), BF16)],
        compiler_params=_params(("parallel", "parallel", "arbitrary")),
        name="band_attn_d%d" % d,
    )(*args)
    return [o.reshape(n, cq) for o in outs]


def _merge_kernel(o1, m1, l1, o2, m2, l2, o3, m3, l3, out_ref):
    ms = [m1[...], m2[...], m3[...]]
    m_all = jnp.maximum(jnp.maximum(ms[0], ms[1]), ms[2])
    num = None
    den = None
    for o, m, l in zip((o1, o2, o3), ms, (l1, l2, l3)):
        w = l[...] * jnp.exp(m - m_all)
        num = w * o[...] if num is None else num + w * o[...]
        den = w if den is None else den + w
    out_ref[...] = (num / den).astype(out_ref.dtype)


def _merge_branches(parts, tm):
    n, c = parts[0].shape
    spec = pl.BlockSpec((tm, c), lambda i: (i, 0))
    return pl.pallas_call(
        _merge_kernel,
        grid=(n // tm,),
        in_specs=[spec] * 9,
        out_specs=spec,
        out_shape=jax.ShapeDtypeStruct((n, c), BF16),
        compiler_params=_params(("parallel",)),
        name="merge_branches",
    )(*parts)


def _gla_kernel(q_ref, k_ref, v_ref, r_ref, g_ref, wg_ref, bg_ref, ng_ref, o_ref, st_ref, *,
                chunks_per_step):
    c = B_CHUNK

    @pl.when(pl.program_id(1) == 0)
    def _():
        st_ref[...] = jnp.zeros_like(st_ref)

    gate = _dot(g_ref[0].astype(BF16), wg_ref[...]) + bg_ref[...]
    log_a = jax.nn.log_sigmoid(gate) / B_GATE_TAU
    ri = lax.broadcasted_iota(jnp.int32, (c, c), 0)
    ci = lax.broadcasted_iota(jnp.int32, (c, c), 1)
    causal = ri >= ci
    tri = causal.astype(F32)
    lane = lax.broadcasted_iota(jnp.int32, (c, LANES), 1)
    low = lane < B_DK
    lane_sq = lax.broadcasted_iota(jnp.int32, (LANES, LANES), 1)
    low_sq = lane_sq < B_DK
    norm_g = ng_ref[...]
    n_pairs = B_HEADS // 2

    for ch in range(chunks_per_step):
        rows = slice(ch * c, (ch + 1) * c)
        la = log_a[rows]
        bcum = jnp.dot(tri, la, preferred_element_type=F32,
                       precision=lax.Precision.HIGHEST)
        b_last = bcum[c - 1:c, :]
        b_mid = bcum[c // 2 - 1:c // 2, :]
        qf = q_ref[0, rows, :] * (B_DK ** -0.5)
        kf = k_ref[0, rows, :]
        q_in = (qf * jnp.exp(bcum - b_mid)).astype(BF16)
        k_in = (kf * jnp.exp(b_mid - bcum)).astype(BF16)
        q_st = (qf * jnp.exp(bcum)).astype(BF16)
        k_st = (kf * jnp.exp(b_last - bcum)).astype(BF16)
        decay = jnp.exp(b_last)
        for pr in range(n_pairs):
            pc = slice(pr * LANES, (pr + 1) * LANES)
            st = st_ref[pr]
            st_b = st.astype(BF16)
            d_st = []
            for e in range(2):
                h = 2 * pr + e
                msk = low if e == 0 else jnp.logical_not(low)
                vc = v_ref[0, rows, h * B_DV:(h + 1) * B_DV].astype(BF16)
                zeros = jnp.zeros((c, LANES), BF16)
                att = _dot_nt(jnp.where(msk, q_in[:, pc], zeros), k_in[:, pc])
                att = jnp.where(causal, att, 0.0)
                o = _dot(att.astype(BF16), vc)
                o = o + _dot_nt(jnp.where(msk, q_st[:, pc], zeros), st_b)
                o = _rms(o, norm_g)
                rg = r_ref[0, rows, h * B_DV:(h + 1) * B_DV]
                o_ref[0, rows, h * B_DV:(h + 1) * B_DV] = (o * jax.nn.silu(rg)).astype(o_ref.dtype)
                d_st.append(_dot_tn(vc, k_st[:, pc]))
            st_ref[pr] = st * decay[:, pc] + jnp.where(low_sq, d_st[0], d_st[1])


def _gla(qb, kb, vb, rb, gb, w_gate, b_gate, norm_g, batch):
    n = qb.shape[0]
    seq = n // batch
    t = min(512, seq)
    r3 = lambda a: a.reshape(batch, seq, a.shape[1])
    blk = lambda c: pl.BlockSpec((1, t, c), lambda b, i: (b, i, 0))
    full = lambda a: pl.BlockSpec(a.shape, lambda b, i: (0,) * a.ndim)
    dk2 = B_HEADS * B_DK
    dv2 = B_HEADS * B_DV
    out = pl.pallas_call(
        functools.partial(_gla_kernel, chunks_per_step=t // B_CHUNK),
        grid=(batch, seq // t),
        in_specs=[blk(dk2), blk(dk2), blk(dv2), blk(dv2), blk(LANES),
                  full(w_gate), full(b_gate), full(norm_g)],
        out_specs=blk(dv2),
        out_shape=jax.ShapeDtypeStruct((batch, seq, dv2), BF16),
        scratch_shapes=[pltpu.VMEM((B_HEADS // 2, B_DV, 2 * B_DK), F32)],
        compiler_params=_params(("parallel", "arbitrary")),
        name="gla",
    )(r3(qb), r3(kb), r3(vb), r3(rb), r3(gb), w_gate, b_gate, norm_g)
    return out.reshape(n, dv2)


def _out_proj_kernel(a_ref, b_ref, wa_ref, wb_ref, x_ref, gpost_ref, gpre_ref, x_out, h_out):
    y = _dot(a_ref[...], wa_ref[...]) + _dot(b_ref[...], wb_ref[...])
    x_new = x_ref[...] + _rms(y, gpost_ref[...])
    x_out[...] = x_new
    h_out[...] = _rms(x_new, gpre_ref[...]).astype(h_out.dtype)


def _out_proj(a, b, wa, wb, x2d, g_post, g_pre, tm):
    n, d = x2d.shape
    ca, cb = a.shape[1], b.shape[1]
    row = lambda c: pl.BlockSpec((tm, c), lambda i: (i, 0))
    const = lambda r, c: pl.BlockSpec((r, c), lambda i: (0, 0))
    return pl.pallas_call(
        _out_proj_kernel,
        grid=(n // tm,),
        in_specs=[row(ca), row(cb), const(ca, d), const(cb, d), row(d), const(1, d), const(1, d)],
        out_specs=[row(d), row(d)],
        out_shape=[jax.ShapeDtypeStruct((n, d), F32), jax.ShapeDtypeStruct((n, d), BF16)],
        compiler_params=_params(("parallel",)),
        name="out_proj",
    )(a, b, wa, wb, x2d, g_post.reshape(1, d), g_pre.reshape(1, d))


def _ffn_kernel(h_ref, wg_ref, wu_ref, wd_ref, x_ref, gpost_ref, x_out, acc_ref):
    j = pl.program_id(1)
    h = h_ref[...]
    g = _dot(h, wg_ref[...])
    u = _dot(h, wu_ref[...])
    part = _dot((jax.nn.silu(g) * u).astype(BF16), wd_ref[...])

    @pl.when(j == 0)
    def _():
        acc_ref[...] = part

    @pl.when(j > 0)
    def _():
        acc_ref[...] += part

    @pl.when(j == pl.num_programs(1) - 1)
    def _():
        x_out[...] = x_ref[...] + _rms(acc_ref[...], gpost_ref[...])


def _ffn(h, wg, wu, wd, x2d, g_post, tm, tf):
    n, d = x2d.shape
    dff = wg.shape[1]
    return pl.pallas_call(
        _ffn_kernel,
        grid=(n // tm, dff // tf),
        in_specs=[pl.BlockSpec((tm, d), lambda i, j: (i, 0)),
                  pl.BlockSpec((d, tf), lambda i, j: (0, j)),
                  pl.BlockSpec((d, tf), lambda i, j: (0, j)),
                  pl.BlockSpec((tf, d), lambda i, j: (j, 0)),
                  pl.BlockSpec((tm, d), lambda i, j: (i, 0)),
                  pl.BlockSpec((1, d), lambda i, j: (0, 0))],
        out_specs=pl.BlockSpec((tm, d), lambda i, j: (i, 0)),
        out_shape=jax.ShapeDtypeStruct((n, d), F32),
        scratch_shapes=[pltpu.VMEM((tm, d), F32)],
        compiler_params=_params(("parallel", "arbitrary")),
        name="ffn",
    )(h, wg, wu, wd, x2d, g_post.reshape(1, d))


def _mla_prep_kernel(cq_ref, ckv_ref, kr_ref, qn_ref, kvn_ref, wq_ref, wk_ref, wv_ref,
                     cosq_ref, sinq_ref, cosk_ref, sink_ref, q_out, k_out, v_out, *, scale):
    hq = _rms(cq_ref[...], qn_ref[...]).astype(BF16)
    hkv = _rms(ckv_ref[...], kvn_ref[...]).astype(BF16)
    lane = lax.broadcasted_iota(jnp.int32, (cq_ref.shape[0], LANES), 1)
    half = D_ROPE // 2

    def swap_halves(t, start):
        return jnp.where(lane < start + half, pltpu.roll(t, LANES - half, 1), pltpu.roll(t, half, 1))

    kr = kr_ref[...]
    kpe = kr * cosk_ref[...] + swap_halves(kr, 0) * sink_ref[...]
    kpe = pltpu.roll(kpe, D_NOPE, 1)
    cosq = cosq_ref[...]
    sinq = sinq_ref[...]
    for h in range(D_HEADS):
        cols = slice(h * LANES, (h + 1) * LANES)
        q = _dot(hq, wq_ref[:, cols])
        q = q * cosq + swap_halves(q, D_NOPE) * sinq
        q_out[:, cols] = (q * scale).astype(q_out.dtype)
        k = _dot(hkv, wk_ref[:, cols])
        k_out[:, cols] = (k + kpe).astype(k_out.dtype)
    v_out[...] = _dot(hkv, wv_ref[...]).astype(v_out.dtype)


def _mla_prep(c_q, c_kv, kr, q_norm, kv_norm, wq, wk, wv, tables, seq, tm):
    n = c_q.shape[0]
    tiles_per_seq = seq // tm
    row = lambda c: pl.BlockSpec((tm, c), lambda i: (i, 0))
    const = lambda a: pl.BlockSpec(a.shape, lambda i: (0,) * a.ndim)
    tab = pl.BlockSpec((tm, LANES), lambda i: (i % tiles_per_seq, 0))
    hw = D_HEADS * LANES
    return pl.pallas_call(
        functools.partial(_mla_prep_kernel, scale=float((D_NOPE + D_ROPE) ** -0.5)),
        grid=(n // tm,),
        in_specs=[row(D_Q_RANK), row(D_KV_RANK), row(LANES), const(q_norm), const(kv_norm),
                  const(wq), const(wk), const(wv), tab, tab, tab, tab],
        out_specs=[row(hw), row(hw), row(D_HEADS * D_V)],
        out_shape=[jax.ShapeDtypeStruct((n, hw), BF16), jax.ShapeDtypeStruct((n, hw), BF16),
                   jax.ShapeDtypeStruct((n, D_HEADS * D_V), BF16)],
        compiler_params=_params(("parallel",)),
        name="mla_prep",
    )(c_q, c_kv, kr, q_norm, kv_norm, wq, wk, wv, *tables)


def _rope_tables(seq):
    half = D_ROPE // 2
    freqs = ROPE_BASE ** (-jnp.arange(half, dtype=F32) / half)
    ang = jnp.arange(seq).astype(F32)[:, None] * freqs[None, :]
    cos, sin = jnp.cos(ang), jnp.sin(ang)
    ones = jnp.ones((seq, D_NOPE), F32)
    zeros = lambda c: jnp.zeros((seq, c), F32)
    pad = LANES - D_NOPE - D_ROPE
    cosq = jnp.concatenate([ones, cos, cos, zeros(pad)], axis=1)
    sinq = jnp.concatenate([zeros(D_NOPE), -sin, sin, zeros(pad)], axis=1)
    cosk = jnp.concatenate([cos, cos, zeros(LANES - D_ROPE)], axis=1)
    sink = jnp.concatenate([-sin, sin, zeros(LANES - D_ROPE)], axis=1)
    return cosq, sinq, cosk, sink


def _mla_attn_kernel(q_ref, k_ref, v_ref, o_ref, *, tq, tk):
    qi = pl.program_id(2)
    lane = lax.broadcasted_iota(jnp.int32, (tq, LANES), 1)
    low = lane < D_V
    qs = [q_ref[0, :, e * LANES:(e + 1) * LANES] for e in range(2)]

    def tile(j, carry, masked):
        k0 = pl.multiple_of(j * tk, tk)
        vblk = v_ref[0, pl.ds(k0, tk), :]
        new = []
        for e in range(2):
            m, l, acc = carry[e]
            kblk = k_ref[0, pl.ds(k0, tk), e * LANES:(e + 1) * LANES]
            s = _dot_nt(qs[e], kblk)
            if masked:
                qpos = qi * tq + lax.broadcasted_iota(jnp.int32, (tq, tk), 0)
                kpos = k0 + lax.broadcasted_iota(jnp.int32, (tq, tk), 1)
                s = jnp.where(qpos >= kpos, s, NEG_INF)
            m_new = jnp.maximum(m, jnp.max(s, axis=-1, keepdims=True))
            alpha = jnp.exp(m - m_new)
            p = jnp.exp(s - m_new)
            l = alpha * l + jnp.sum(p, axis=-1, keepdims=True)
            acc = alpha * acc + _dot(p.astype(BF16), vblk)
            new.append((m_new, l, acc))
        return tuple(new)

    init = tuple((jnp.full((tq, 1), NEG_INF, F32), jnp.zeros((tq, 1), F32),
                  jnp.zeros((tq, LANES), F32)) for _ in range(2))
    n_full = (qi * tq) // tk
    carry = lax.fori_loop(0, n_full, lambda j, c: tile(j, c, False), init)
    for d in range(tq // tk):
        carry = tile(n_full + d, carry, True)
    outs = [acc / l for (_, l, acc) in carry]
    o_ref[0] = jnp.where(low, outs[0], outs[1]).astype(o_ref.dtype)


def _mla_attn(q, k, v, batch, tq, tk):
    n = q.shape[0]
    seq = n // batch
    n_pairs = D_HEADS // 2
    q3 = q.reshape(batch, seq, q.shape[1])
    k3 = k.reshape(batch, seq, k.shape[1])
    v3 = v.reshape(batch, seq, v.shape[1])
    out = pl.pallas_call(
        functools.partial(_mla_attn_kernel, tq=tq, tk=tk),
        grid=(batch, n_pairs, seq // tq),
        in_specs=[pl.BlockSpec((1, tq, 2 * LANES), lambda b, p, i: (b, i, p)),
                  pl.BlockSpec((1, seq, 2 * LANES), lambda b, p, i: (b, 0, p)),
                  pl.BlockSpec((1, seq, LANES), lambda b, p, i: (b, 0, p))],
        out_specs=pl.BlockSpec((1, tq, LANES), lambda b, p, i: (b, i, p)),
        out_shape=jax.ShapeDtypeStruct((batch, seq, n_pairs * LANES), BF16),
        compiler_params=_params(("parallel", "parallel", "arbitrary")),
        name="mla_attn",
    )(q3, k3, v3)
    return out.reshape(n, n_pairs * LANES)


def _pad_cols(w, width):
    return jnp.pad(w, ((0, 0), (0, width - w.shape[1])))


def _layer0_w_in(w):
    return _pad_cols(w, 25 * LANES).astype(BF16)


def _layer1_w_in(w):
    hd = HEAD_DIM
    qc = w[:, :C_HEADS * hd]
    o = C_HEADS * hd
    kc = w[:, o:o + C_KV_HEADS * hd]
    o += C_KV_HEADS * hd
    vc = w[:, o:o + C_KV_HEADS * hd]
    o += C_KV_HEADS * hd
    rest = w[:, o:o + D_Q_RANK + D_KV_RANK]
    o += D_Q_RANK + D_KV_RANK
    kr = _pad_cols(w[:, o:o + D_ROPE], LANES)
    rep = lambda t: jnp.concatenate([t[:, g * hd:(g + 1) * hd] for g in range(C_KV_HEADS) for _ in range(2)], axis=1)
    return jnp.concatenate([qc, rep(kc), rep(vc), rest, kr], axis=1).astype(BF16)


def _mla_weights(w_uq, w_ukv):
    dq = D_NOPE + D_ROPE
    wq = w_uq.reshape(D_Q_RANK, D_HEADS, dq)
    wq = jnp.pad(wq, ((0, 0), (0, 0), (0, LANES - dq))).reshape(D_Q_RANK, D_HEADS * LANES)
    wkv = w_ukv.reshape(D_KV_RANK, D_HEADS, D_NOPE + D_V)
    wk = jnp.pad(wkv[:, :, :D_NOPE], ((0, 0), (0, 0), (0, LANES - D_NOPE)))
    wk = wk.reshape(D_KV_RANK, D_HEADS * LANES)
    wv = wkv[:, :, D_NOPE:].reshape(D_KV_RANK, D_HEADS * D_V)
    return wq.astype(BF16), wk.astype(BF16), wv.astype(BF16)


def _row_tile(n, want):
    return min(want, n)


def _mix_even(x2d, batch, g_pre, w_in, gla_w_gate_up, gla_b_gate, gla_norm):
    n = x2d.shape[0]
    ah = A_HEADS * HEAD_DIM
    dk2, dv2 = B_HEADS * B_DK, B_HEADS * B_DV
    sizes = (ah, ah, ah, dk2, dk2, dv2, dv2, LANES)
    scales = (HEAD_DIM ** -0.5, 1.0, 1.0, 1.0, 1.0, 1.0, 1.0, 1.0)
    starts = np.concatenate([[0], np.cumsum(sizes)[:-1]])
    splits = tuple((int(s), int(z), float(c)) for s, z, c in zip(starts, sizes, scales))
    dtypes = (BF16, BF16, BF16, F32, F32, F32, F32, F32)
    qa, ka, va, qb, kb, vb, rb, gb = _norm_proj(x2d, g_pre, _layer0_w_in(w_in), splits, dtypes,
                                                _row_tile(n, 512))
    slopes = _alibi_slopes(A_HEADS)
    parts = []
    for window, dil in A_PATTERNS:
        parts += _band_attn(qa, ka, va, batch=batch, dilation=dil, max_rel=window // dil,
                            slopes=slopes, kv_of_pair=tuple(range(A_HEADS // 2)), with_stats=True)
    o_a = _merge_branches(parts, _row_tile(n, 1024))
    w_gate = jnp.pad(gla_w_gate_up, ((0, LANES - B_GATE_RANK), (0, 0))).astype(BF16)
    o_b = _gla(qb, kb, vb, rb, gb, w_gate, gla_b_gate.reshape(1, dk2), gla_norm.reshape(1, B_DV), batch)
    return o_a, o_b


def _mix_odd(x2d, batch, g_pre, w_in, sinks, q_norm, w_uq, kv_norm, w_ukv):
    n = x2d.shape[0]
    seq = n // batch
    ch = C_HEADS * HEAD_DIM
    ckv = 2 * C_KV_HEADS * HEAD_DIM
    sizes = (ch, ckv, ckv, D_Q_RANK, D_KV_RANK, LANES)
    scales = (HEAD_DIM ** -0.5, 1.0, 1.0, 1.0, 1.0, 1.0)
    starts = np.concatenate([[0], np.cumsum(sizes)[:-1]])
    splits = tuple((int(s), int(z), float(c)) for s, z, c in zip(starts, sizes, scales))
    dtypes = (BF16, BF16, BF16, F32, F32, F32)
    qc, kc, vc, c_q, c_kv, kr = _norm_proj(x2d, g_pre, _layer1_w_in(w_in), splits, dtypes,
                                           _row_tile(n, 512))
    (o_c,) = _band_attn(qc, kc, vc, batch=batch, dilation=1, max_rel=C_WINDOW - 1,
                        slopes=_alibi_slopes(C_HEADS),
                        kv_of_pair=tuple(hp // 2 for hp in range(C_HEADS // 2)),
                        sinks=sinks.astype(F32), out_dtype=BF16)
    wq, wk, wv = _mla_weights(w_uq, w_ukv)
    q, k, v = _mla_prep(c_q, c_kv, kr, q_norm.reshape(1, D_Q_RANK), kv_norm.reshape(1, D_KV_RANK),
                        wq, wk, wv, _rope_tables(seq), seq, _row_tile(seq, 512))
    o_d = _mla_attn(q, k, v, batch, _row_tile(seq, 512), _row_tile(seq, 512))
    return o_c, o_d


def kernel(x, norm_mix_pre, norm_mix_post, norm_ffn_pre, norm_ffn_post, ffn_w_gate, ffn_w_up, ffn_w_down, ab_w_in, ab_w_out, gla_w_gate_up, gla_b_gate, gla_norm, cd_w_in, cd_w_out, swa_sinks, mla_q_norm, mla_w_uq, mla_kv_norm, mla_w_ukv):
    batch, seq, d = x.shape
    n = batch * seq
    x2d = x.reshape(n, d)
    depth = norm_mix_pre.shape[0]
    for layer in range(depth):
        i = layer // 2
        if layer % 2 == 0:
            o1, o2 = _mix_even(x2d, batch, norm_mix_pre[layer], ab_w_in[i], gla_w_gate_up[i],
                               gla_b_gate[i], gla_norm[i])
            w_out = ab_w_out[i].astype(BF16)
        else:
            o1, o2 = _mix_odd(x2d, batch, norm_mix_pre[layer], cd_w_in[i], swa_sinks[i],
                              mla_q_norm[i], mla_w_uq[i], mla_kv_norm[i], mla_w_ukv[i])
            w_out = cd_w_out[i].astype(BF16)
        c1 = o1.shape[1]
        x2d, h = _out_proj(o1, o2, w_out[:c1], w_out[c1:], x2d, norm_mix_post[layer],
                           norm_ffn_pre[layer], _row_tile(n, 512))
        x2d = _ffn(h, ffn_w_gate[layer].astype(BF16), ffn_w_up[layer].astype(BF16),
                   ffn_w_down[layer].astype(BF16), x2d, norm_ffn_post[layer],
                   _row_tile(n, 1024), 256)
    return x2d.reshape(batch, seq, d)
```

```python
import functools

import numpy as np
import jax
import jax.numpy as jnp
from jax import lax
from jax.experimental import pallas as pl
from jax.experimental.pallas import tpu as pltpu

F32 = jnp.float32
BF16 = jnp.bfloat16

LANES = 128
V7X_VMEM_BYTES = 64 * 1024 * 1024
VMEM_LIMIT = 56 * 1024 * 1024

D_MODEL = 1024
HEAD_DIM = 64
Q_BLOCK = 128
EPS = 1e-6
NEG_INF = -1e30
A_HEADS = 8
A_PATTERNS = ((128, 1), (512, 4), (2048, 16))
B_HEADS = 4
B_DK = 64
B_DV = 128
B_GATE_RANK = 16
B_GATE_TAU = 16.0
B_CHUNK = 64
C_HEADS = 8
C_KV_HEADS = 2
C_WINDOW = 128
D_HEADS = 8
D_Q_RANK = 384
D_KV_RANK = 256
D_NOPE = 64
D_ROPE = 32
D_V = 64
ROPE_BASE = 10000.0
D_FF = 2816


def _alibi_slopes(n):
    return [float(np.float32(2.0 ** (-8.0 * (i + 1) / n))) for i in range(n)]


def _params(semantics):
    return pltpu.CompilerParams(dimension_semantics=semantics, vmem_limit_bytes=VMEM_LIMIT)


def _rms(x, g):
    return x * lax.rsqrt(jnp.mean(x * x, axis=-1, keepdims=True) + EPS) * g


def _dot(a, b):
    return jnp.dot(a, b, preferred_element_type=F32)


def _dot_nt(a, b):
    return lax.dot_general(a, b, (((1,), (1,)), ((), ())), preferred_element_type=F32)


def _dot_tn(a, b):
    return lax.dot_general(a, b, (((0,), (0,)), ((), ())), preferred_element_type=F32)


def _norm_proj_kernel(x_ref, g_ref, w_ref, *out_refs, splits):
    h = _rms(x_ref[...], g_ref[...]).astype(BF16)
    for o_ref, (start, size, scale) in zip(out_refs, splits):
        y = _dot(h, w_ref[:, start:start + size])
        if scale != 1.0:
            y = y * scale
        o_ref[...] = y.astype(o_ref.dtype)


def _norm_proj(x2d, g, w, splits, dtypes, tm):
    n, d = x2d.shape
    wn = w.shape[1]
    out_shape = [jax.ShapeDtypeStruct((n, size), dt) for (_, size, _), dt in zip(splits, dtypes)]
    out_specs = [pl.BlockSpec((tm, size), lambda i: (i, 0)) for (_, size, _) in splits]
    return pl.pallas_call(
        functools.partial(_norm_proj_kernel, splits=splits),
        grid=(n // tm,),
        in_specs=[pl.BlockSpec((tm, d), lambda i: (i, 0)),
                  pl.BlockSpec((1, d), lambda i: (0, 0)),
                  pl.BlockSpec((d, wn), lambda i: (0, 0))],
        out_specs=out_specs,
        out_shape=out_shape,
        compiler_params=_params(("parallel",)),
        name="norm_proj",
    )(x2d, g.reshape(1, d), w)


def _band_attn_kernel(*refs, n_pairs, kv_of_pair, slopes, dilation, max_rel, tq,
                      with_sinks, with_stats):
    if with_sinks:
        sink_ref, refs = refs[0], refs[1:]
    q_ref, kc_ref, kp_ref, vc_ref, vp_ref = refs[:5]
    if with_stats:
        o_ref, m_ref, l_ref, kcat, vcat = refs[5:]
    else:
        o_ref, kcat, vcat = refs[5:]

    kcat[0:Q_BLOCK, :] = kp_ref[0]
    kcat[Q_BLOCK:, :] = kc_ref[0]
    vcat[0:Q_BLOCK, :] = vp_ref[0]
    vcat[Q_BLOCK:, :] = vc_ref[0]

    first_tile = pl.program_id(2) == 0
    row = lax.broadcasted_iota(jnp.int32, (Q_BLOCK, 2 * Q_BLOCK), 0)
    col = lax.broadcasted_iota(jnp.int32, (Q_BLOCK, 2 * Q_BLOCK), 1)
    rel_i = row + Q_BLOCK - col
    band = (rel_i >= 0) & (rel_i <= max_rel)
    dist = (rel_i * dilation).astype(F32)
    lane = lax.broadcasted_iota(jnp.int32, (Q_BLOCK, LANES), 1)
    low = lane < HEAD_DIM

    def q_block(t, carry):
        r0 = pl.multiple_of(t * Q_BLOCK, Q_BLOCK)
        valid = band & ((col >= Q_BLOCK) | jnp.logical_not(first_tile & (t == 0)))
        for hp in range(n_pairs):
            qblk = q_ref[0, pl.ds(r0, Q_BLOCK), hp * LANES:(hp + 1) * LANES]
            kv = kv_of_pair[hp]
            kblk = kcat[pl.ds(r0, 2 * Q_BLOCK), kv * LANES:(kv + 1) * LANES]
            vblk = vcat[pl.ds(r0, 2 * Q_BLOCK), kv * LANES:(kv + 1) * LANES]
            outs, ms, ls = [], [], []
            for e in range(2):
                h = 2 * hp + e
                qm = jnp.where(low if e == 0 else jnp.logical_not(low), qblk, jnp.zeros_like(qblk))
                s = _dot_nt(qm, kblk)
                s = jnp.where(valid, s - slopes[h] * dist, NEG_INF)
                m = jnp.max(s, axis=-1, keepdims=True)
                if with_sinks:
                    sink = sink_ref[h]
                    m = jnp.maximum(m, sink)
                p = jnp.exp(s - m)
                den = jnp.sum(p, axis=-1, keepdims=True)
                if with_sinks:
                    den = den + jnp.exp(sink - m)
                o = _dot(p.astype(BF16), vblk) / den
                outs.append(o)
                ms.append(m)
                ls.append(den)
            cols = slice(hp * LANES, (hp + 1) * LANES)
            o_ref[0, pl.ds(r0, Q_BLOCK), cols] = jnp.where(low, outs[0], outs[1]).astype(o_ref.dtype)
            if with_stats:
                m_ref[0, pl.ds(r0, Q_BLOCK), cols] = jnp.where(low, ms[0], ms[1])
                l_ref[0, pl.ds(r0, Q_BLOCK), cols] = jnp.where(low, ls[0], ls[1])
        return carry

    lax.fori_loop(0, tq // Q_BLOCK, q_block, 0)


def _band_attn(q, k, v, *, batch, dilation, max_rel, slopes, kv_of_pair, sinks=None,
               with_stats=False, out_dtype=F32):
    n, cq = q.shape
    ck = k.shape[1]
    seq = n // batch
    d = dilation
    length = seq // d
    tq = min(1024, length)
    n_pairs = cq // LANES
    nb_per_tile = tq // Q_BLOCK
    qv = q.reshape(batch, length, d * cq)
    kv = k.reshape(batch, length, d * ck)
    vv = v.reshape(batch, length, d * ck)
    cur = lambda b, r, i: (b, i, r)
    prev = lambda b, r, i: (b, jnp.maximum(i * nb_per_tile - 1, 0), r)
    in_specs = [pl.BlockSpec((1, tq, cq), cur),
                pl.BlockSpec((1, tq, ck), cur),
                pl.BlockSpec((1, Q_BLOCK, ck), prev),
                pl.BlockSpec((1, tq, ck), cur),
                pl.BlockSpec((1, Q_BLOCK, ck), prev)]
    args = [qv, kv, kv, vv, vv]
    if sinks is not None:
        in_specs = [pl.BlockSpec(memory_space=pltpu.SMEM)] + in_specs
        args = [sinks] + args
    o_spec = pl.BlockSpec((1, tq, cq), cur)
    out_shape = [jax.ShapeDtypeStruct((batch, length, d * cq), out_dtype)]
    out_specs = [o_spec]
    if with_stats:
        out_shape += [jax.ShapeDtypeStruct((batch, length, d * cq), F32)] * 2
        out_specs += [o_spec, o_spec]
    outs = pl.pallas_call(
        functools.partial(_band_attn_kernel, n_pairs=n_pairs, kv_of_pair=kv_of_pair,
                          slopes=slopes, dilation=d, max_rel=max_rel, tq=tq,
                          with_sinks=sinks is not None, with_stats=with_stats),
        grid=(batch, d, length // tq),
        in_specs=in_specs,
        out_specs=out_specs,
        out_shape=out_shape,
        scratch_shapes=[pltpu.VMEM((tq + Q_BLOCK, ck), BF16),
                        pltpu.VMEM((tq + Q_BLOCK, ck), BF16)],
        compiler_params=_params(("parallel", "parallel", "arbitrary")),
        name="band_attn_d%d" % d,
    )(*args)
    return [o.reshape(n, cq) for o in outs]


def _merge_kernel(o1, m1, l1, o2, m2, l2, o3, m3, l3, out_ref):
    ms = [m1[...], m2[...], m3[...]]
    m_all = jnp.maximum(jnp.maximum(ms[0], ms[1]), ms[2])
    num = None
    den = None
    for o, m, l in zip((o1, o2, o3), ms, (l1, l2, l3)):
        w = l[...] * jnp.exp(m - m_all)
        num = w * o[...] if num is None else num + w * o[...]
        den = w if den is None else den + w
    out_ref[...] = (num / den).astype(out_ref.dtype)


def _merge_branches(parts, tm):
    n, c = parts[0].shape
    spec = pl.BlockSpec((tm, c), lambda i: (i, 0))
    return pl.pallas_call(
        _merge_kernel,
        grid=(n // tm,),
        in_specs=[spec] * 9,
        out_specs=spec,
        out_shape=jax.ShapeDtypeStruct((n, c), BF16),
        compiler_params=_params(("parallel",)),
        name="merge_branches",
    )(*parts)


def _gla_kernel(q_ref, k_ref, v_ref, r_ref, g_ref, wg_ref, bg_ref, ng_ref, o_ref, st_ref, *,
                chunks_per_step):
    c = B_CHUNK

    @pl.when(pl.program_id(1) == 0)
    def _():
        st_ref[...] = jnp.zeros_like(st_ref)

    gate = _dot(g_ref[0].astype(BF16), wg_ref[...]) + bg_ref[...]
    log_a = jax.nn.log_sigmoid(gate) / B_GATE_TAU
    ri = lax.broadcasted_iota(jnp.int32, (c, c), 0)
    ci = lax.broadcasted_iota(jnp.int32, (c, c), 1)
    causal = ri >= ci
    tri = causal.astype(F32)
    lane = lax.broadcasted_iota(jnp.int32, (c, LANES), 1)
    low = lane < B_DK
    lane_sq = lax.broadcasted_iota(jnp.int32, (LANES, LANES), 1)
    low_sq = lane_sq < B_DK
    norm_g = ng_ref[...]
    n_pairs = B_HEADS // 2

    for ch in range(chunks_per_step):
        rows = slice(ch * c, (ch + 1) * c)
        la = log_a[rows]
        bcum = jnp.dot(tri, la, preferred_element_type=F32,
                       precision=lax.Precision.HIGHEST)
        b_last = bcum[c - 1:c, :]
        b_mid = bcum[c // 2 - 1:c // 2, :]
        qf = q_ref[0, rows, :] * (B_DK ** -0.5)
        kf = k_ref[0, rows, :]
        q_in = (qf * jnp.exp(bcum - b_mid)).astype(BF16)
        k_in = (kf * jnp.exp(b_mid - bcum)).astype(BF16)
        q_st = (qf * jnp.exp(bcum)).astype(BF16)
        k_st = (kf * jnp.exp(b_last - bcum)).astype(BF16)
        decay = jnp.exp(b_last)
        for pr in range(n_pairs):
            pc = slice(pr * LANES, (pr + 1) * LANES)
            st = st_ref[pr]
            st_b = st.astype(BF16)
            d_st = []
            for e in range(2):
                h = 2 * pr + e
                msk = low if e == 0 else jnp.logical_not(low)
                vc = v_ref[0, rows, h * B_DV:(h + 1) * B_DV].astype(BF16)
                zeros = jnp.zeros((c, LANES), BF16)
                att = _dot_nt(jnp.where(msk, q_in[:, pc], zeros), k_in[:, pc])
                att = jnp.where(causal, att, 0.0)
                o = _dot(att.astype(BF16), vc)
                o = o + _dot_nt(jnp.where(msk, q_st[:, pc], zeros), st_b)
                o = _rms(o, norm_g)
                rg = r_ref[0, rows, h * B_DV:(h + 1) * B_DV]
                o_ref[0, rows, h * B_DV:(h + 1) * B_DV] = (o * jax.nn.silu(rg)).astype(o_ref.dtype)
                d_st.append(_dot_tn(vc, k_st[:, pc]))
            st_ref[pr] = st * decay[:, pc] + jnp.where(low_sq, d_st[0], d_st[1])


def _gla(qb, kb, vb, rb, gb, w_gate, b_gate, norm_g, batch):
    n = qb.shape[0]
    seq = n // batch
    t = min(512, seq)
    r3 = lambda a: a.reshape(batch, seq, a.shape[1])
    blk = lambda c: pl.BlockSpec((1, t, c), lambda b, i: (b, i, 0))
    full = lambda a: pl.BlockSpec(a.shape, lambda b, i: (0,) * a.ndim)
    dk2 = B_HEADS * B_DK
    dv2 = B_HEADS * B_DV
    out = pl.pallas_call(
        functools.partial(_gla_kernel, chunks_per_step=t // B_CHUNK),
        grid=(batch, seq // t),
        in_specs=[blk(dk2), blk(dk2), blk(dv2), blk(dv2), blk(LANES),
                  full(w_gate), full(b_gate), full(norm_g)],
        out_specs=blk(dv2),
        out_shape=jax.ShapeDtypeStruct((batch, seq, dv2), BF16),
        scratch_shapes=[pltpu.VMEM((B_HEADS // 2, B_DV, 2 * B_DK), F32)],
        compiler_params=_params(("parallel", "arbitrary")),
        name="gla",
    )(r3(qb), r3(kb), r3(vb), r3(rb), r3(gb), w_gate, b_gate, norm_g)
    return out.reshape(n, dv2)


def _out_proj_kernel(a_ref, b_ref, wa_ref, wb_ref, x_ref, gpost_ref, gpre_ref, x_out, h_out):
    y = _dot(a_ref[...], wa_ref[...]) + _dot(b_ref[...], wb_ref[...])
    x_new = x_ref[...] + _rms(y, gpost_ref[...])
    x_out[...] = x_new
    h_out[...] = _rms(x_new, gpre_ref[...]).astype(h_out.dtype)


def _out_proj(a, b, wa, wb, x2d, g_post, g_pre, tm):
    n, d = x2d.shape
    ca, cb = a.shape[1], b.shape[1]
    row = lambda c: pl.BlockSpec((tm, c), lambda i: (i, 0))
    const = lambda r, c: pl.BlockSpec((r, c), lambda i: (0, 0))
    return pl.pallas_call(
        _out_proj_kernel,
        grid=(n // tm,),
        in_specs=[row(ca), row(cb), const(ca, d), const(cb, d), row(d), const(1, d), const(1, d)],
        out_specs=[row(d), row(d)],
        out_shape=[jax.ShapeDtypeStruct((n, d), F32), jax.ShapeDtypeStruct((n, d), BF16)],
        compiler_params=_params(("parallel",)),
        name="out_proj",
    )(a, b, wa, wb, x2d, g_post.reshape(1, d), g_pre.reshape(1, d))


def _ffn_kernel(h_ref, wg_ref, wu_ref, wd_ref, x_ref, gpost_ref, x_out, *, tf):
    h = h_ref[...]
    f = None
    for c in range(wg_ref.shape[1] // tf):
        cols = slice(c * tf, (c + 1) * tf)
        g = _dot(h, wg_ref[:, cols])
        u = _dot(h, wu_ref[:, cols])
        part = _dot((jax.nn.silu(g) * u).astype(BF16), wd_ref[cols, :])
        f = part if f is None else f + part
    x_out[...] = x_ref[...] + _rms(f, gpost_ref[...])


def _ffn(h, wg, wu, wd, x2d, g_post, tm, tf):
    n, d = x2d.shape
    dff = wg.shape[1]
    resident = lambda r, c: pl.BlockSpec((r, c), lambda i: (0, 0), pipeline_mode=pl.Buffered(1))
    return pl.pallas_call(
        functools.partial(_ffn_kernel, tf=tf),
        grid=(n // tm,),
        in_specs=[pl.BlockSpec((tm, d), lambda i: (i, 0)),
                  resident(d, dff), resident(d, dff), resident(dff, d),
                  pl.BlockSpec((tm, d), lambda i: (i, 0)),
                  pl.BlockSpec((1, d), lambda i: (0, 0))],
        out_specs=pl.BlockSpec((tm, d), lambda i: (i, 0)),
        out_shape=jax.ShapeDtypeStruct((n, d), F32),
        compiler_params=_params(("parallel",)),
        name="ffn",
    )(h, wg, wu, wd, x2d, g_post.reshape(1, d))


def _mla_prep_kernel(cq_ref, ckv_ref, kr_ref, qn_ref, kvn_ref, wq_ref, wk_ref, wv_ref,
                     cosq_ref, sinq_ref, cosk_ref, sink_ref, q_out, k_out, v_out, *, scale):
    hq = _rms(cq_ref[...], qn_ref[...]).astype(BF16)
    hkv = _rms(ckv_ref[...], kvn_ref[...]).astype(BF16)
    lane = lax.broadcasted_iota(jnp.int32, (cq_ref.shape[0], LANES), 1)
    half = D_ROPE // 2

    def swap_halves(t, start):
        return jnp.where(lane < start + half, pltpu.roll(t, LANES - half, 1), pltpu.roll(t, half, 1))

    kr = kr_ref[...]
    kpe = kr * cosk_ref[...] + swap_halves(kr, 0) * sink_ref[...]
    kpe = pltpu.roll(kpe, D_NOPE, 1)
    cosq = cosq_ref[...]
    sinq = sinq_ref[...]
    for h in range(D_HEADS):
        cols = slice(h * LANES, (h + 1) * LANES)
        q = _dot(hq, wq_ref[:, cols])
        q = q * cosq + swap_halves(q, D_NOPE) * sinq
        q_out[:, cols] = (q * scale).astype(q_out.dtype)
        k = _dot(hkv, wk_ref[:, cols])
        k_out[:, cols] = (k + kpe).astype(k_out.dtype)
    for pr in range(D_HEADS // 2):
        v = _dot(hkv, wv_ref[:, pr * LANES:(pr + 1) * LANES])
        v_out[:, 2 * pr * LANES:(2 * pr + 1) * LANES] = v.astype(v_out.dtype)
        v_out[:, (2 * pr + 1) * LANES:(2 * pr + 2) * LANES] = jnp.ones(v.shape, v_out.dtype)


def _mla_prep(c_q, c_kv, kr, q_norm, kv_norm, wq, wk, wv, tables, seq, tm):
    n = c_q.shape[0]
    tiles_per_seq = seq // tm
    row = lambda c: pl.BlockSpec((tm, c), lambda i: (i, 0))
    const = lambda a: pl.BlockSpec(a.shape, lambda i: (0,) * a.ndim)
    tab = pl.BlockSpec((tm, LANES), lambda i: (i % tiles_per_seq, 0))
    hw = D_HEADS * LANES
    return pl.pallas_call(
        functools.partial(_mla_prep_kernel, scale=float((D_NOPE + D_ROPE) ** -0.5 * np.log2(np.e))),
        grid=(n // tm,),
        in_specs=[row(D_Q_RANK), row(D_KV_RANK), row(LANES), const(q_norm), const(kv_norm),
                  const(wq), const(wk), const(wv), tab, tab, tab, tab],
        out_specs=[row(hw), row(hw), row(hw)],
        out_shape=[jax.ShapeDtypeStruct((n, hw), BF16)] * 3,
        compiler_params=_params(("parallel",)),
        name="mla_prep",
    )(c_q, c_kv, kr, q_norm, kv_norm, wq, wk, wv, *tables)


def _rope_tables(seq):
    half = D_ROPE // 2
    freqs = ROPE_BASE ** (-jnp.arange(half, dtype=F32) / half)
    ang = jnp.arange(seq).astype(F32)[:, None] * freqs[None, :]
    cos, sin = jnp.cos(ang), jnp.sin(ang)
    ones = jnp.ones((seq, D_NOPE), F32)
    zeros = lambda c: jnp.zeros((seq, c), F32)
    pad = LANES - D_NOPE - D_ROPE
    cosq = jnp.concatenate([ones, cos, cos, zeros(pad)], axis=1)
    sinq = jnp.concatenate([zeros(D_NOPE), -sin, sin, zeros(pad)], axis=1)
    cosk = jnp.concatenate([cos, cos, zeros(LANES - D_ROPE)], axis=1)
    sink = jnp.concatenate([-sin, sin, zeros(LANES - D_ROPE)], axis=1)
    return cosq, sinq, cosk, sink


def _mla_attn_kernel(q_ref, k_ref, v_ref, o_ref, s_ref, p_ref, m_ref, l_ref, a_ref, acc_ref, *,
                     tq, tk, rb):
    qi = pl.program_id(2)
    n_lane_tiles = tk // LANES
    m_ref[...] = jnp.full(m_ref.shape, NEG_INF, F32)
    l_ref[...] = jnp.zeros(l_ref.shape, F32)
    acc_ref[...] = jnp.zeros(acc_ref.shape, F32)
    qs = [q_ref[0, :, e * LANES:(e + 1) * LANES] for e in range(2)]

    def softmax_rows(e, k0, masked, buf):
        for r in range(tq // rb):
            rows = slice(r * rb, (r + 1) * rb)
            s = s_ref[buf, e, rows, :]
            if masked:
                qpos = qi * tq + r * rb + lax.broadcasted_iota(jnp.int32, (rb, tk), 0)
                kpos = k0 + lax.broadcasted_iota(jnp.int32, (rb, tk), 1)
                s = jnp.where(qpos >= kpos, s, NEG_INF)
            m_old = m_ref[e, rows, :]
            m_new = jnp.maximum(m_old, jnp.max(s, axis=-1, keepdims=True))
            alpha = jnp.exp2(m_old - m_new)
            p = jnp.exp2(s - jnp.concatenate([m_new] * n_lane_tiles, axis=1))
            m_ref[e, rows, :] = m_new
            a_ref[buf, e, rows, :] = alpha
            p_ref[buf, e, rows, :] = p.astype(BF16)

    def scores(t, buf):
        k0 = pl.multiple_of(t * tk, tk)
        for e in range(2):
            kblk = k_ref[0, pl.ds(k0, tk), e * LANES:(e + 1) * LANES]
            s_ref[buf, e] = _dot_nt(qs[e], kblk)

    def softmax(t, buf, masked):
        for e in range(2):
            softmax_rows(e, t * tk, masked, buf)

    def values(t, buf):
        k0 = pl.multiple_of(jnp.maximum(t, 0) * tk, tk)
        vblk = v_ref[0, pl.ds(k0, tk), :]
        for e in range(2):
            pv = _dot(p_ref[buf, e], vblk)
            alpha = a_ref[buf, e]
            acc_ref[e] = alpha * acc_ref[e] + pv[:, :LANES]
            l_ref[e] = alpha * l_ref[e] + pv[:, LANES:]

    p_ref[1] = jnp.zeros(p_ref.shape[1:], BF16)
    a_ref[1] = jnp.zeros(a_ref.shape[1:], F32)

    def step(t, buf):
        scores(t + 1, 1 - buf)
        values(t - 1, 1 - buf)
        softmax(t, buf, False)

    def finish(buf):
        values(n_full - 1, 1 - buf)
        softmax(n_full, buf, True)
        values(n_full, buf)

    n_full = (qi * tq) // tk
    scores(0, 0)

    def body(jj, carry):
        step(2 * jj, 0)
        step(2 * jj + 1, 1)
        return carry

    lax.fori_loop(0, n_full // 2, body, 0)

    @pl.when(n_full % 2 == 0)
    def _():
        finish(0)

    @pl.when(n_full % 2 == 1)
    def _():
        step(n_full - 1, 0)
        finish(1)

    lane = lax.broadcasted_iota(jnp.int32, (tq, LANES), 1)
    outs = [acc_ref[e] / l_ref[e] for e in range(2)]
    o_ref[0] = jnp.where(lane < D_V, outs[0], outs[1]).astype(o_ref.dtype)


def _mla_attn(q, k, v, batch, tq, tk):
    n = q.shape[0]
    seq = n // batch
    n_pairs = D_HEADS // 2
    q3 = q.reshape(batch, seq, q.shape[1])
    k3 = k.reshape(batch, seq, k.shape[1])
    v3 = v.reshape(batch, seq, v.shape[1])
    stat = pltpu.VMEM((2, tq, LANES), F32)
    out = pl.pallas_call(
        functools.partial(_mla_attn_kernel, tq=tq, tk=tk, rb=16),
        grid=(batch, n_pairs, seq // tq),
        in_specs=[pl.BlockSpec((1, tq, 2 * LANES), lambda b, p, i: (b, i, p)),
                  pl.BlockSpec((1, seq, 2 * LANES), lambda b, p, i: (b, 0, p)),
                  pl.BlockSpec((1, seq, 2 * LANES), lambda b, p, i: (b, 0, p))],
        out_specs=pl.BlockSpec((1, tq, LANES), lambda b, p, i: (b, i, p)),
        out_shape=jax.ShapeDtypeStruct((batch, seq, n_pairs * LANES), BF16),
        scratch_shapes=[pltpu.VMEM((2, 2, tq, tk), F32), pltpu.VMEM((2, 2, tq, tk), BF16),
                        stat, stat, pltpu.VMEM((2, 2, tq, LANES), F32), stat],
        compiler_params=_params(("parallel", "parallel", "arbitrary")),
        name="mla_attn",
    )(q3, k3, v3)
    return out.reshape(n, n_pairs * LANES)


def _pad_cols(w, width):
    return jnp.pad(w, ((0, 0), (0, width - w.shape[1])))


def _layer0_w_in(w):
    return _pad_cols(w, 25 * LANES).astype(BF16)


def _layer1_w_in(w):
    hd = HEAD_DIM
    qc = w[:, :C_HEADS * hd]
    o = C_HEADS * hd
    kc = w[:, o:o + C_KV_HEADS * hd]
    o += C_KV_HEADS * hd
    vc = w[:, o:o + C_KV_HEADS * hd]
    o += C_KV_HEADS * hd
    rest = w[:, o:o + D_Q_RANK + D_KV_RANK]
    o += D_Q_RANK + D_KV_RANK
    kr = _pad_cols(w[:, o:o + D_ROPE], LANES)
    rep = lambda t: jnp.concatenate([t[:, g * hd:(g + 1) * hd] for g in range(C_KV_HEADS) for _ in range(2)], axis=1)
    return jnp.concatenate([qc, rep(kc), rep(vc), rest, kr], axis=1).astype(BF16)


def _mla_weights(w_uq, w_ukv):
    dq = D_NOPE + D_ROPE
    wq = w_uq.reshape(D_Q_RANK, D_HEADS, dq)
    wq = jnp.pad(wq, ((0, 0), (0, 0), (0, LANES - dq))).reshape(D_Q_RANK, D_HEADS * LANES)
    wkv = w_ukv.reshape(D_KV_RANK, D_HEADS, D_NOPE + D_V)
    wk = jnp.pad(wkv[:, :, :D_NOPE], ((0, 0), (0, 0), (0, LANES - D_NOPE)))
    wk = wk.reshape(D_KV_RANK, D_HEADS * LANES)
    wv = wkv[:, :, D_NOPE:].reshape(D_KV_RANK, D_HEADS * D_V)
    return wq.astype(BF16), wk.astype(BF16), wv.astype(BF16)


def _row_tile(n, want):
    return min(want, n)


def _mix_even(x2d, batch, g_pre, w_in, gla_w_gate_up, gla_b_gate, gla_norm):
    n = x2d.shape[0]
    ah = A_HEADS * HEAD_DIM
    dk2, dv2 = B_HEADS * B_DK, B_HEADS * B_DV
    sizes = (ah, ah, ah, dk2, dk2, dv2, dv2, LANES)
    scales = (HEAD_DIM ** -0.5, 1.0, 1.0, 1.0, 1.0, 1.0, 1.0, 1.0)
    starts = np.concatenate([[0], np.cumsum(sizes)[:-1]])
    splits = tuple((int(s), int(z), float(c)) for s, z, c in zip(starts, sizes, scales))
    dtypes = (BF16, BF16, BF16, F32, F32, F32, F32, F32)
    qa, ka, va, qb, kb, vb, rb, gb = _norm_proj(x2d, g_pre, _layer0_w_in(w_in), splits, dtypes,
                                                _row_tile(n, 512))
    slopes = _alibi_slopes(A_HEADS)
    parts = []
    for window, dil in A_PATTERNS:
        parts += _band_attn(qa, ka, va, batch=batch, dilation=dil, max_rel=window // dil,
                            slopes=slopes, kv_of_pair=tuple(range(A_HEADS // 2)), with_stats=True)
    o_a = _merge_branches(parts, _row_tile(n, 1024))
    w_gate = jnp.pad(gla_w_gate_up, ((0, LANES - B_GATE_RANK), (0, 0))).astype(BF16)
    o_b = _gla(qb, kb, vb, rb, gb, w_gate, gla_b_gate.reshape(1, dk2), gla_norm.reshape(1, B_DV), batch)
    return o_a, o_b


def _mix_odd(x2d, batch, g_pre, w_in, sinks, q_norm, w_uq, kv_norm, w_ukv):
    n = x2d.shape[0]
    seq = n // batch
    ch = C_HEADS * HEAD_DIM
    ckv = 2 * C_KV_HEADS * HEAD_DIM
    sizes = (ch, ckv, ckv, D_Q_RANK, D_KV_RANK, LANES)
    scales = (HEAD_DIM ** -0.5, 1.0, 1.0, 1.0, 1.0, 1.0)
    starts = np.concatenate([[0], np.cumsum(sizes)[:-1]])
    splits = tuple((int(s), int(z), float(c)) for s, z, c in zip(starts, sizes, scales))
    dtypes = (BF16, BF16, BF16, F32, F32, F32)
    qc, kc, vc, c_q, c_kv, kr = _norm_proj(x2d, g_pre, _layer1_w_in(w_in), splits, dtypes,
                                           _row_tile(n, 512))
    (o_c,) = _band_attn(qc, kc, vc, batch=batch, dilation=1, max_rel=C_WINDOW - 1,
                        slopes=_alibi_slopes(C_HEADS),
                        kv_of_pair=tuple(hp // 2 for hp in range(C_HEADS // 2)),
                        sinks=sinks.astype(F32), out_dtype=BF16)
    wq, wk, wv = _mla_weights(w_uq, w_ukv)
    q, k, v = _mla_prep(c_q, c_kv, kr, q_norm.reshape(1, D_Q_RANK), kv_norm.reshape(1, D_KV_RANK),
                        wq, wk, wv, _rope_tables(seq), seq, _row_tile(seq, 512))
    o_d = _mla_attn(q, k, v, batch, _row_tile(seq, 512), _row_tile(seq, 1024))
    return o_c, o_d


def kernel(x, norm_mix_pre, norm_mix_post, norm_ffn_pre, norm_ffn_post, ffn_w_gate, ffn_w_up, ffn_w_down, ab_w_in, ab_w_out, gla_w_gate_up, gla_b_gate, gla_norm, cd_w_in, cd_w_out, swa_sinks, mla_q_norm, mla_w_uq, mla_kv_norm, mla_w_ukv):
    batch, seq, d = x.shape
    n = batch * seq
    x2d = x.reshape(n, d)
    depth = norm_mix_pre.shape[0]
    for layer in range(depth):
        i = layer // 2
        if layer % 2 == 0:
            o1, o2 = _mix_even(x2d, batch, norm_mix_pre[layer], ab_w_in[i], gla_w_gate_up[i],
                               gla_b_gate[i], gla_norm[i])
            w_out = ab_w_out[i].astype(BF16)
        else:
            o1, o2 = _mix_odd(x2d, batch, norm_mix_pre[layer], cd_w_in[i], swa_sinks[i],
                              mla_q_norm[i], mla_w_uq[i], mla_kv_norm[i], mla_w_ukv[i])
            w_out = cd_w_out[i].astype(BF16)
        c1 = o1.shape[1]
        x2d, h = _out_proj(o1, o2, w_out[:c1], w_out[c1:], x2d, norm_mix_post[layer],
                           norm_ffn_pre[layer], _row_tile(n, 512))
        x2d = _ffn(h, ffn_w_gate[layer].astype(BF16), ffn_w_up[layer].astype(BF16),
                   ffn_w_down[layer].astype(BF16), x2d, norm_ffn_post[layer],
                   _row_tile(n, 512), 256)
    return x2d.reshape(batch, seq, d)
```

```python
import functools

import numpy as np
import jax
import jax.numpy as jnp
from jax import lax
from jax.experimental import pallas as pl
from jax.experimental.pallas import tpu as pltpu

F32 = jnp.float32
BF16 = jnp.bfloat16

LANES = 128
V7X_VMEM_BYTES = 64 * 1024 * 1024
VMEM_LIMIT = 56 * 1024 * 1024
LOG2E = float(np.log2(np.e))

D_MODEL = 1024
HEAD_DIM = 64
Q_BLOCK = 128
EPS = 1e-6
NEG_INF = -1e30
A_HEADS = 8
A_PATTERNS = ((128, 1), (512, 4), (2048, 16))
B_HEADS = 4
B_DK = 64
B_DV = 128
B_GATE_RANK = 16
B_GATE_TAU = 16.0
B_CHUNK = 64
C_HEADS = 8
C_KV_HEADS = 2
C_WINDOW = 128
D_HEADS = 8
D_Q_RANK = 384
D_KV_RANK = 256
D_NOPE = 64
D_ROPE = 32
D_V = 64
ROPE_BASE = 10000.0
D_FF = 2816


def _alibi_slopes(n):
    return np.array([2.0 ** (-8.0 * (i + 1) / n) for i in range(n)], dtype=np.float32)


def _params(semantics):
    return pltpu.CompilerParams(dimension_semantics=semantics, vmem_limit_bytes=VMEM_LIMIT)


def _rms(x, g):
    return x * lax.rsqrt(jnp.mean(x * x, axis=-1, keepdims=True) + EPS) * g


def _dot(a, b):
    return jnp.dot(a, b, preferred_element_type=F32)


def _dot_nt(a, b):
    return lax.dot_general(a, b, (((1,), (1,)), ((), ())), preferred_element_type=F32)


def _dot_tn(a, b):
    return lax.dot_general(a, b, (((0,), (0,)), ((), ())), preferred_element_type=F32)


def _norm_proj_kernel(x_ref, g_ref, w_ref, *out_refs, splits):
    h = _rms(x_ref[...], g_ref[...]).astype(BF16)
    for o_ref, (start, size, scale, slabs) in zip(out_refs, splits):
        y = _dot(h, w_ref[:, start:start + size])
        if scale != 1.0:
            y = y * scale
        if slabs:
            for j in range(size // LANES):
                o_ref[j] = y[:, j * LANES:(j + 1) * LANES].astype(o_ref.dtype)
        else:
            o_ref[...] = y.astype(o_ref.dtype)


def _norm_proj(x2d, g, w, splits, dtypes, tm):
    n, d = x2d.shape
    wn = w.shape[1]
    out_shape, out_specs = [], []
    for (_, size, _, slabs), dt in zip(splits, dtypes):
        if slabs:
            out_shape.append(jax.ShapeDtypeStruct((size // LANES, n, LANES), dt))
            out_specs.append(pl.BlockSpec((size // LANES, tm, LANES), lambda i: (0, i, 0)))
        else:
            out_shape.append(jax.ShapeDtypeStruct((n, size), dt))
            out_specs.append(pl.BlockSpec((tm, size), lambda i: (i, 0)))
    return pl.pallas_call(
        functools.partial(_norm_proj_kernel, splits=splits),
        grid=(n // tm,),
        in_specs=[pl.BlockSpec((tm, d), lambda i: (i, 0)),
                  pl.BlockSpec((1, d), lambda i: (0, 0)),
                  pl.BlockSpec((d, wn), lambda i: (0, 0))],
        out_specs=out_specs,
        out_shape=out_shape,
        compiler_params=_params(("parallel",)),
        name="norm_proj",
    )(x2d, g.reshape(1, d), w)


def _band_attn_kernel(slope_ref, m0_ref, l0_ref, q_ref, kc_ref, kp_ref, vc_ref, vp_ref, o_ref,
                      kcat, vcat, acc_ref, m_ref, l_ref, bias_ref, *, patterns, tt):
    hp = pl.program_id(1)
    first_tile = pl.program_id(2) == 0
    kcat[0:tt, :] = kp_ref[0]
    kcat[tt:, :] = kc_ref[0]
    vcat[0:tt, :] = vp_ref[0]
    vcat[tt:, :] = vc_ref[0]
    for e in range(2):
        m_ref[e] = jnp.full((tt, LANES), m0_ref[2 * hp + e], F32)
        l_ref[e] = jnp.full((tt, LANES), l0_ref[2 * hp + e], F32)
    acc_ref[...] = jnp.zeros(acc_ref.shape, F32)

    row = lax.broadcasted_iota(jnp.int32, (Q_BLOCK, 2 * Q_BLOCK), 0)
    col = lax.broadcasted_iota(jnp.int32, (Q_BLOCK, 2 * Q_BLOCK), 1)
    rel = row + Q_BLOCK - col
    lane = lax.broadcasted_iota(jnp.int32, (Q_BLOCK, LANES), 1)
    low = lane < HEAD_DIM
    ones = jnp.ones((2 * Q_BLOCK, LANES), BF16)

    for max_rel, d in patterns:
        band = (rel >= 0) & (rel <= max_rel)
        dist = (rel * d).astype(F32)
        for e in range(2):
            bias_ref[e] = slope_ref[2 * hp + e] * dist
        blocks_per_residue = tt // (Q_BLOCK * d)

        for idx in range(tt // Q_BLOCK):
            r = idx // blocks_per_residue
            t = idx % blocks_per_residue
            q0 = r + t * (Q_BLOCK * d)
            k0 = tt - Q_BLOCK * d + q0
            qrows = pl.ds(q0, Q_BLOCK, stride=d)
            krows = pl.ds(k0, 2 * Q_BLOCK, stride=d)
            valid = (band & ((col >= Q_BLOCK) | jnp.logical_not(first_tile))) if t == 0 else band
            qblk = q_ref[0, qrows, :].astype(BF16)
            kblk = kcat[krows, :].astype(BF16)
            vext = jnp.concatenate([vcat[krows, :].astype(BF16), ones], axis=1)
            alphas, pvs = [], []
            for e in range(2):
                qm = jnp.where(low if e == 0 else jnp.logical_not(low), qblk, jnp.zeros_like(qblk))
                s = _dot_nt(qm, kblk)
                s = jnp.where(valid, s - bias_ref[e], NEG_INF)
                m_old = m_ref[e, qrows, :]
                m_new = jnp.maximum(m_old, jnp.max(s, axis=-1, keepdims=True))
                alpha = jnp.exp2(m_old - m_new)
                p = jnp.exp2(s - jnp.concatenate([m_new, m_new], axis=1))
                pv = _dot(p.astype(BF16), vext)
                m_ref[e, qrows, :] = m_new
                l_ref[e, qrows, :] = alpha * l_ref[e, qrows, :] + pv[:, LANES:]
                alphas.append(alpha)
                pvs.append(pv[:, :LANES])
            acc_ref[qrows, :] = (jnp.where(low, alphas[0], alphas[1]) * acc_ref[qrows, :]
                                 + jnp.where(low, pvs[0], pvs[1]))

    lane_t = lax.broadcasted_iota(jnp.int32, (tt, LANES), 1)
    o_ref[...] = (acc_ref[...] / jnp.where(lane_t < HEAD_DIM, l_ref[0], l_ref[1])).astype(o_ref.dtype)


def _band_attn(q, k, v, *, batch, patterns, slopes, m0, l0):
    n_pairs, n, _ = q.shape
    n_kv = k.shape[0]
    seq = n // batch
    max_d = max(d for _, d in patterns)
    tt = min(seq, max(2048, Q_BLOCK * max_d))
    tiles = seq // tt
    cur = lambda b, p, i: (p, b * tiles + i, 0)
    cur_kv = lambda b, p, i: (p * n_kv // n_pairs, b * tiles + i, 0)
    prev_kv = lambda b, p, i: (p * n_kv // n_pairs, b * tiles + jnp.maximum(i - 1, 0), 0)
    smem = pl.BlockSpec(memory_space=pltpu.SMEM)
    blk = lambda im: pl.BlockSpec((1, tt, LANES), im)
    stat = pltpu.VMEM((2, tt, LANES), F32)
    return pl.pallas_call(
        functools.partial(_band_attn_kernel, patterns=patterns, tt=tt),
        grid=(batch, n_pairs, tiles),
        in_specs=[smem, smem, smem, blk(cur), blk(cur_kv), blk(prev_kv), blk(cur_kv), blk(prev_kv)],
        out_specs=pl.BlockSpec((tt, LANES), lambda b, p, i: (b * tiles + i, p)),
        out_shape=jax.ShapeDtypeStruct((n, n_pairs * LANES), BF16),
        scratch_shapes=[pltpu.VMEM((2 * tt, LANES), F32), pltpu.VMEM((2 * tt, LANES), F32),
                        pltpu.VMEM((tt, LANES), F32), stat, stat,
                        pltpu.VMEM((2, Q_BLOCK, 2 * Q_BLOCK), F32)],
        compiler_params=_params(("parallel", "parallel", "arbitrary")),
        name="band_attn_p%d" % len(patterns),
    )(slopes, m0, l0, q, k, k, v, v)


def _gla_kernel(q_ref, k_ref, v_ref, r_ref, g_ref, wg_ref, bg_ref, ng_ref, o_ref, st_ref, *,
                chunks_per_step):
    c = B_CHUNK

    @pl.when(pl.program_id(1) == 0)
    def _():
        st_ref[...] = jnp.zeros_like(st_ref)

    gate = _dot(g_ref[0].astype(BF16), wg_ref[...]) + bg_ref[...]
    log_a = jax.nn.log_sigmoid(gate) / B_GATE_TAU
    ri = lax.broadcasted_iota(jnp.int32, (c, c), 0)
    ci = lax.broadcasted_iota(jnp.int32, (c, c), 1)
    causal = ri >= ci
    tri = causal.astype(F32)
    lane = lax.broadcasted_iota(jnp.int32, (c, LANES), 1)
    low = lane < B_DK
    lane_sq = lax.broadcasted_iota(jnp.int32, (LANES, LANES), 1)
    low_sq = lane_sq < B_DK
    norm_g = ng_ref[...]
    n_pairs = B_HEADS // 2

    for ch in range(chunks_per_step):
        rows = slice(ch * c, (ch + 1) * c)
        la = log_a[rows]
        bcum = jnp.dot(tri, la, preferred_element_type=F32,
                       precision=lax.Precision.HIGHEST)
        b_last = bcum[c - 1:c, :]
        b_mid = bcum[c // 2 - 1:c // 2, :]
        qf = q_ref[0, rows, :] * (B_DK ** -0.5)
        kf = k_ref[0, rows, :]
        q_in = (qf * jnp.exp(bcum - b_mid)).astype(BF16)
        k_in = (kf * jnp.exp(b_mid - bcum)).astype(BF16)
        q_st = (qf * jnp.exp(bcum)).astype(BF16)
        k_st = (kf * jnp.exp(b_last - bcum)).astype(BF16)
        decay = jnp.exp(b_last)
        for pr in range(n_pairs):
            pc = slice(pr * LANES, (pr + 1) * LANES)
            st = st_ref[pr]
            st_b = st.astype(BF16)
            d_st = []
            for e in range(2):
                h = 2 * pr + e
                msk = low if e == 0 else jnp.logical_not(low)
                vc = v_ref[0, rows, h * B_DV:(h + 1) * B_DV].astype(BF16)
                zeros = jnp.zeros((c, LANES), BF16)
                att = _dot_nt(jnp.where(msk, q_in[:, pc], zeros), k_in[:, pc])
                att = jnp.where(causal, att, 0.0)
                o = _dot(att.astype(BF16), vc)
                o = o + _dot_nt(jnp.where(msk, q_st[:, pc], zeros), st_b)
                o = _rms(o, norm_g)
                rg = r_ref[0, rows, h * B_DV:(h + 1) * B_DV]
                o_ref[0, rows, h * B_DV:(h + 1) * B_DV] = (o * jax.nn.silu(rg)).astype(o_ref.dtype)
                d_st.append(_dot_tn(vc, k_st[:, pc]))
            st_ref[pr] = st * decay[:, pc] + jnp.where(low_sq, d_st[0], d_st[1])


def _gla(qb, kb, vb, rb, gb, w_gate, b_gate, norm_g, batch):
    n = qb.shape[0]
    seq = n // batch
    t = min(512, seq)
    r3 = lambda a: a.reshape(batch, seq, a.shape[1])
    blk = lambda c: pl.BlockSpec((1, t, c), lambda b, i: (b, i, 0))
    full = lambda a: pl.BlockSpec(a.shape, lambda b, i: (0,) * a.ndim)
    dk2 = B_HEADS * B_DK
    dv2 = B_HEADS * B_DV
    out = pl.pallas_call(
        functools.partial(_gla_kernel, chunks_per_step=t // B_CHUNK),
        grid=(batch, seq // t),
        in_specs=[blk(dk2), blk(dk2), blk(dv2), blk(dv2), blk(LANES),
                  full(w_gate), full(b_gate), full(norm_g)],
        out_specs=blk(dv2),
        out_shape=jax.ShapeDtypeStruct((batch, seq, dv2), BF16),
        scratch_shapes=[pltpu.VMEM((B_HEADS // 2, B_DV, 2 * B_DK), F32)],
        compiler_params=_params(("parallel", "arbitrary")),
        name="gla",
    )(r3(qb), r3(kb), r3(vb), r3(rb), r3(gb), w_gate, b_gate, norm_g)
    return out.reshape(n, dv2)


def _proj_ffn_kernel(a_ref, b_ref, wa_ref, wb_ref, x_ref, gmix_ref, gpre_ref, wg_ref, wu_ref, wd_ref,
                     gffn_ref, x_out, *, tf):
    y = _dot(a_ref[...], wa_ref[...]) + _dot(b_ref[...], wb_ref[...])
    x1 = x_ref[...] + _rms(y, gmix_ref[...])
    h = _rms(x1, gpre_ref[...]).astype(BF16)
    f = None
    for c in range(wg_ref.shape[1] // tf):
        cols = slice(c * tf, (c + 1) * tf)
        g = _dot(h, wg_ref[:, cols])
        u = _dot(h, wu_ref[:, cols])
        part = _dot((jax.nn.silu(g) * u).astype(BF16), wd_ref[cols, :])
        f = part if f is None else f + part
    x_out[...] = x1 + _rms(f, gffn_ref[...])


def _proj_ffn(a, b, wa, wb, x2d, g_mix, g_pre, wg, wu, wd, g_ffn, tm, tf):
    n, d = x2d.shape
    ca, cb = a.shape[1], b.shape[1]
    dff = wg.shape[1]
    row = lambda c: pl.BlockSpec((tm, c), lambda i: (i, 0))
    resident = lambda r, c: pl.BlockSpec((r, c), lambda i: (0, 0), pipeline_mode=pl.Buffered(1))
    vec = lambda g: g.reshape(1, d)
    return pl.pallas_call(
        functools.partial(_proj_ffn_kernel, tf=tf),
        grid=(n // tm,),
        in_specs=[row(ca), row(cb), resident(ca, d), resident(cb, d), row(d), resident(1, d),
                  resident(1, d), resident(d, dff), resident(d, dff), resident(dff, d), resident(1, d)],
        out_specs=row(d),
        out_shape=jax.ShapeDtypeStruct((n, d), F32),
        compiler_params=_params(("parallel",)),
        name="proj_ffn",
    )(a, b, wa, wb, x2d, vec(g_mix), vec(g_pre), wg, wu, wd, vec(g_ffn))


def _mla_prep_kernel(cq_ref, ckv_ref, kr_ref, qn_ref, kvn_ref, wq_ref, wk_ref, wv_ref,
                     cosq_ref, sinq_ref, cosk_ref, sink_ref, q_out, k_out, v_out, *, scale):
    hq = _rms(cq_ref[...], qn_ref[...]).astype(BF16)
    hkv = _rms(ckv_ref[...], kvn_ref[...]).astype(BF16)
    lane = lax.broadcasted_iota(jnp.int32, (cq_ref.shape[0], LANES), 1)
    half = D_ROPE // 2

    def swap_halves(t, start):
        return jnp.where(lane < start + half, pltpu.roll(t, LANES - half, 1), pltpu.roll(t, half, 1))

    kr = kr_ref[...]
    kpe = kr * cosk_ref[...] + swap_halves(kr, 0) * sink_ref[...]
    kpe = pltpu.roll(kpe, D_NOPE, 1)
    cosq = cosq_ref[...]
    sinq = sinq_ref[...]
    for h in range(D_HEADS):
        cols = slice(h * LANES, (h + 1) * LANES)
        q = _dot(hq, wq_ref[:, cols])
        q = q * cosq + swap_halves(q, D_NOPE) * sinq
        q_out[:, cols] = (q * scale).astype(q_out.dtype)
        k = _dot(hkv, wk_ref[:, cols])
        k_out[:, cols] = (k + kpe).astype(k_out.dtype)
    for pr in range(D_HEADS // 2):
        v = _dot(hkv, wv_ref[:, pr * LANES:(pr + 1) * LANES])
        v_out[:, 2 * pr * LANES:(2 * pr + 1) * LANES] = v.astype(v_out.dtype)
        v_out[:, (2 * pr + 1) * LANES:(2 * pr + 2) * LANES] = jnp.ones(v.shape, v_out.dtype)


def _mla_prep(c_q, c_kv, kr, q_norm, kv_norm, wq, wk, wv, tables, seq, tm):
    n = c_q.shape[0]
    tiles_per_seq = seq // tm
    row = lambda c: pl.BlockSpec((tm, c), lambda i: (i, 0))
    const = lambda a: pl.BlockSpec(a.shape, lambda i: (0,) * a.ndim)
    tab = pl.BlockSpec((tm, LANES), lambda i: (i % tiles_per_seq, 0))
    hw = D_HEADS * LANES
    return pl.pallas_call(
        functools.partial(_mla_prep_kernel, scale=float((D_NOPE + D_ROPE) ** -0.5) * LOG2E),
        grid=(n // tm,),
        in_specs=[row(D_Q_RANK), row(D_KV_RANK), row(LANES), const(q_norm), const(kv_norm),
                  const(wq), const(wk), const(wv), tab, tab, tab, tab],
        out_specs=[row(hw), row(hw), row(hw)],
        out_shape=[jax.ShapeDtypeStruct((n, hw), BF16)] * 3,
        compiler_params=_params(("parallel",)),
        name="mla_prep",
    )(c_q, c_kv, kr, q_norm, kv_norm, wq, wk, wv, *tables)


def _rope_tables(seq):
    half = D_ROPE // 2
    freqs = ROPE_BASE ** (-jnp.arange(half, dtype=F32) / half)
    ang = jnp.arange(seq).astype(F32)[:, None] * freqs[None, :]
    cos, sin = jnp.cos(ang), jnp.sin(ang)
    ones = jnp.ones((seq, D_NOPE), F32)
    zeros = lambda c: jnp.zeros((seq, c), F32)
    pad = LANES - D_NOPE - D_ROPE
    cosq = jnp.concatenate([ones, cos, cos, zeros(pad)], axis=1)
    sinq = jnp.concatenate([zeros(D_NOPE), -sin, sin, zeros(pad)], axis=1)
    cosk = jnp.concatenate([cos, cos, zeros(LANES - D_ROPE)], axis=1)
    sink = jnp.concatenate([-sin, sin, zeros(LANES - D_ROPE)], axis=1)
    return cosq, sinq, cosk, sink


def _mla_attn_kernel(q_ref, k_ref, v_ref, o_ref, s_ref, p_ref, m_ref, l_ref, a_ref, acc_ref, *,
                     tq, tk, rb):
    qi = pl.program_id(2)
    n_lane_tiles = tk // LANES
    m_ref[...] = jnp.full(m_ref.shape, NEG_INF, F32)
    l_ref[...] = jnp.zeros(l_ref.shape, F32)
    acc_ref[...] = jnp.zeros(acc_ref.shape, F32)
    qs = [q_ref[0, :, e * LANES:(e + 1) * LANES] for e in range(2)]

    def softmax_rows(e, k0, masked, buf):
        for r in range(tq // rb):
            rows = slice(r * rb, (r + 1) * rb)
            s = s_ref[buf, e, rows, :]
            if masked:
                qpos = qi * tq + r * rb + lax.broadcasted_iota(jnp.int32, (rb, tk), 0)
                kpos = k0 + lax.broadcasted_iota(jnp.int32, (rb, tk), 1)
                s = jnp.where(qpos >= kpos, s, NEG_INF)
            m_old = m_ref[e, rows, :]
            m_new = jnp.maximum(m_old, jnp.max(s, axis=-1, keepdims=True))
            alpha = jnp.exp2(m_old - m_new)
            p = jnp.exp2(s - jnp.concatenate([m_new] * n_lane_tiles, axis=1))
            m_ref[e, rows, :] = m_new
            a_ref[buf, e, rows, :] = alpha
            p_ref[buf, e, rows, :] = p.astype(BF16)

    def scores(t, buf):
        k0 = pl.multiple_of(t * tk, tk)
        for e in range(2):
            kblk = k_ref[0, pl.ds(k0, tk), e * LANES:(e + 1) * LANES]
            s_ref[buf, e] = _dot_nt(qs[e], kblk)

    def softmax(t, buf, masked):
        for e in range(2):
            softmax_rows(e, t * tk, masked, buf)

    def values(t, buf):
        k0 = pl.multiple_of(t * tk, tk)
        vblk = v_ref[0, pl.ds(k0, tk), :]
        for e in range(2):
            pv = _dot(p_ref[buf, e], vblk)
            alpha = a_ref[buf, e]
            acc_ref[e] = alpha * acc_ref[e] + pv[:, :LANES]
            l_ref[e] = alpha * l_ref[e] + pv[:, LANES:]

    n_full = (qi * tq) // tk
    p_ref[1] = jnp.zeros(p_ref.shape[1:], BF16)
    a_ref[1] = jnp.zeros(a_ref.shape[1:], F32)
    scores(n_full, 0)
    scores(0, 1)
    values(0, 1)
    softmax(n_full, 0, True)

    def step(t, buf):
        scores(t + 1, 1 - buf)
        values(jnp.where(t == 0, n_full, t - 1), 1 - buf)
        softmax(t, buf, False)

    def body(jj, carry):
        step(2 * jj, 1)
        step(2 * jj + 1, 0)
        return carry

    lax.fori_loop(0, n_full // 2, body, 0)

    @pl.when(n_full % 2 == 0)
    def _():
        values(jnp.maximum(n_full - 1, 0), 0)

    @pl.when(n_full % 2 == 1)
    def _():
        step(n_full - 1, 1)
        values(n_full - 1, 1)

    lane = lax.broadcasted_iota(jnp.int32, (tq, LANES), 1)
    outs = [acc_ref[e] / l_ref[e] for e in range(2)]
    o_ref[0] = jnp.where(lane < D_V, outs[0], outs[1]).astype(o_ref.dtype)


def _mla_attn(q, k, v, batch, tq, tk):
    n = q.shape[0]
    seq = n // batch
    n_pairs = D_HEADS // 2
    q3 = q.reshape(batch, seq, q.shape[1])
    k3 = k.reshape(batch, seq, k.shape[1])
    v3 = v.reshape(batch, seq, v.shape[1])
    stat = pltpu.VMEM((2, tq, LANES), F32)
    out = pl.pallas_call(
        functools.partial(_mla_attn_kernel, tq=tq, tk=tk, rb=16),
        grid=(batch, n_pairs, seq // tq),
        in_specs=[pl.BlockSpec((1, tq, 2 * LANES), lambda b, p, i: (b, i, p)),
                  pl.BlockSpec((1, seq, 2 * LANES), lambda b, p, i: (b, 0, p)),
                  pl.BlockSpec((1, seq, 2 * LANES), lambda b, p, i: (b, 0, p))],
        out_specs=pl.BlockSpec((1, tq, LANES), lambda b, p, i: (b, i, p)),
        out_shape=jax.ShapeDtypeStruct((batch, seq, n_pairs * LANES), BF16),
        scratch_shapes=[pltpu.VMEM((2, 2, tq, tk), F32), pltpu.VMEM((2, 2, tq, tk), BF16),
                        stat, stat, pltpu.VMEM((2, 2, tq, LANES), F32), stat],
        compiler_params=_params(("parallel", "parallel", "arbitrary")),
        name="mla_attn",
    )(q3, k3, v3)
    return out.reshape(n, n_pairs * LANES)


def _pad_cols(w, width):
    return jnp.pad(w, ((0, 0), (0, width - w.shape[1])))


def _layer0_w_in(w):
    return _pad_cols(w, 25 * LANES).astype(BF16)


def _layer1_w_in(w):
    hd = HEAD_DIM
    qc = w[:, :C_HEADS * hd]
    o = C_HEADS * hd
    kc = w[:, o:o + C_KV_HEADS * hd]
    o += C_KV_HEADS * hd
    vc = w[:, o:o + C_KV_HEADS * hd]
    o += C_KV_HEADS * hd
    rest = w[:, o:o + D_Q_RANK + D_KV_RANK]
    o += D_Q_RANK + D_KV_RANK
    kr = _pad_cols(w[:, o:o + D_ROPE], LANES)
    rep = lambda t: jnp.concatenate([t[:, g * hd:(g + 1) * hd] for g in range(C_KV_HEADS) for _ in range(2)], axis=1)
    return jnp.concatenate([qc, rep(kc), rep(vc), rest, kr], axis=1).astype(BF16)


def _mla_weights(w_uq, w_ukv):
    dq = D_NOPE + D_ROPE
    wq = w_uq.reshape(D_Q_RANK, D_HEADS, dq)
    wq = jnp.pad(wq, ((0, 0), (0, 0), (0, LANES - dq))).reshape(D_Q_RANK, D_HEADS * LANES)
    wkv = w_ukv.reshape(D_KV_RANK, D_HEADS, D_NOPE + D_V)
    wk = jnp.pad(wkv[:, :, :D_NOPE], ((0, 0), (0, 0), (0, LANES - D_NOPE)))
    wk = wk.reshape(D_KV_RANK, D_HEADS * LANES)
    wv = wkv[:, :, D_NOPE:].reshape(D_KV_RANK, D_HEADS * D_V)
    return wq.astype(BF16), wk.astype(BF16), wv.astype(BF16)


def _row_tile(n, want):
    return min(want, n)


def _splits(sizes, scales, slabs):
    starts = np.concatenate([[0], np.cumsum(sizes)[:-1]])
    return tuple((int(s), int(z), float(c), bool(b)) for s, z, c, b in zip(starts, sizes, scales, slabs))


def _mix_even(x2d, batch, g_pre, w_in, gla_w_gate_up, gla_b_gate, gla_norm):
    n = x2d.shape[0]
    ah = A_HEADS * HEAD_DIM
    dk2, dv2 = B_HEADS * B_DK, B_HEADS * B_DV
    sizes = (ah, ah, ah, dk2, dk2, dv2, dv2, LANES)
    scales = (HEAD_DIM ** -0.5 * LOG2E,) + (1.0,) * 7
    slabs = (True, True, True) + (False,) * 5
    qa, ka, va, qb, kb, vb, rb, gb = _norm_proj(x2d, g_pre, _layer0_w_in(w_in),
                                                _splits(sizes, scales, slabs), (F32,) * 8,
                                                _row_tile(n, 512))
    o_a = _band_attn(qa, ka, va, batch=batch,
                     patterns=tuple((w // d, d) for w, d in A_PATTERNS),
                     slopes=jnp.asarray(_alibi_slopes(A_HEADS) * np.float32(LOG2E)),
                     m0=jnp.full((A_HEADS,), NEG_INF, F32), l0=jnp.zeros((A_HEADS,), F32))
    w_gate = jnp.pad(gla_w_gate_up, ((0, LANES - B_GATE_RANK), (0, 0))).astype(BF16)
    o_b = _gla(qb, kb, vb, rb, gb, w_gate, gla_b_gate.reshape(1, dk2), gla_norm.reshape(1, B_DV), batch)
    return o_a, o_b


def _mix_odd(x2d, batch, g_pre, w_in, sinks, q_norm, w_uq, kv_norm, w_ukv):
    n = x2d.shape[0]
    seq = n // batch
    ch = C_HEADS * HEAD_DIM
    ckv = 2 * C_KV_HEADS * HEAD_DIM
    sizes = (ch, ckv, ckv, D_Q_RANK, D_KV_RANK, LANES)
    scales = (HEAD_DIM ** -0.5 * LOG2E,) + (1.0,) * 5
    slabs = (True, True, True, False, False, False)
    qc, kc, vc, c_q, c_kv, kr = _norm_proj(x2d, g_pre, _layer1_w_in(w_in),
                                           _splits(sizes, scales, slabs), (F32,) * 6,
                                           _row_tile(n, 512))
    o_c = _band_attn(qc, kc, vc, batch=batch, patterns=((C_WINDOW - 1, 1),),
                     slopes=jnp.asarray(_alibi_slopes(C_HEADS) * np.float32(LOG2E)),
                     m0=sinks.astype(F32) * LOG2E, l0=jnp.ones((C_HEADS,), F32))
    wq, wk, wv = _mla_weights(w_uq, w_ukv)
    q, k, v = _mla_prep(c_q, c_kv, kr, q_norm.reshape(1, D_Q_RANK), kv_norm.reshape(1, D_KV_RANK),
                        wq, wk, wv, _rope_tables(seq), seq, _row_tile(seq, 512))
    o_d = _mla_attn(q, k, v, batch, _row_tile(seq, 512), _row_tile(seq, 1024))
    return o_c, o_d


def kernel(x, norm_mix_pre, norm_mix_post, norm_ffn_pre, norm_ffn_post, ffn_w_gate, ffn_w_up, ffn_w_down, ab_w_in, ab_w_out, gla_w_gate_up, gla_b_gate, gla_norm, cd_w_in, cd_w_out, swa_sinks, mla_q_norm, mla_w_uq, mla_kv_norm, mla_w_ukv):
    batch, seq, d = x.shape
    n = batch * seq
    x2d = x.reshape(n, d)
    depth = norm_mix_pre.shape[0]
    for layer in range(depth):
        i = layer // 2
        if layer % 2 == 0:
            o1, o2 = _mix_even(x2d, batch, norm_mix_pre[layer], ab_w_in[i], gla_w_gate_up[i],
                               gla_b_gate[i], gla_norm[i])
            w_out = ab_w_out[i].astype(BF16)
        else:
            o1, o2 = _mix_odd(x2d, batch, norm_mix_pre[layer], cd_w_in[i], swa_sinks[i],
                              mla_q_norm[i], mla_w_uq[i], mla_kv_norm[i], mla_w_ukv[i])
            w_out = cd_w_out[i].astype(BF16)
        c1 = o1.shape[1]
        x2d = _proj_ffn(o1, o2, w_out[:c1], w_out[c1:], x2d, norm_mix_post[layer],
                        norm_ffn_pre[layer], ffn_w_gate[layer].astype(BF16),
                        ffn_w_up[layer].astype(BF16), ffn_w_down[layer].astype(BF16),
                        norm_ffn_post[layer], _row_tile(n, 512), 256)
    return x2d.reshape(batch, seq, d)
```

```python
import functools

import numpy as np
import jax
import jax.numpy as jnp
from jax import lax
from jax.experimental import pallas as pl
from jax.experimental.pallas import tpu as pltpu

F32 = jnp.float32
BF16 = jnp.bfloat16

LANES = 128
V7X_VMEM_BYTES = 64 * 1024 * 1024
VMEM_LIMIT = 56 * 1024 * 1024
LOG2E = float(np.log2(np.e))

D_MODEL = 1024
HEAD_DIM = 64
Q_BLOCK = 128
EPS = 1e-6
NEG_INF = -1e30
A_HEADS = 8
A_PATTERNS = ((128, 1), (512, 4), (2048, 16))
B_HEADS = 4
B_DK = 64
B_DV = 128
B_GATE_RANK = 16
B_GATE_TAU = 16.0
B_CHUNK = 64
C_HEADS = 8
C_KV_HEADS = 2
C_WINDOW = 128
D_HEADS = 8
D_Q_RANK = 384
D_KV_RANK = 256
D_NOPE = 64
D_ROPE = 32
D_V = 64
ROPE_BASE = 10000.0
D_FF = 2816


def _alibi_slopes(n):
    return np.array([2.0 ** (-8.0 * (i + 1) / n) for i in range(n)], dtype=np.float32)


def _params(semantics):
    return pltpu.CompilerParams(dimension_semantics=semantics, vmem_limit_bytes=VMEM_LIMIT)


def _rms(x, g):
    return x * lax.rsqrt(jnp.mean(x * x, axis=-1, keepdims=True) + EPS) * g


def _dot(a, b):
    return jnp.dot(a, b, preferred_element_type=F32)


def _dot_nt(a, b):
    return lax.dot_general(a, b, (((1,), (1,)), ((), ())), preferred_element_type=F32)


def _dot_tn(a, b):
    return lax.dot_general(a, b, (((0,), (0,)), ((), ())), preferred_element_type=F32)


def _norm_proj_kernel(x_ref, g_ref, w_ref, *out_refs, splits):
    h = _rms(x_ref[...], g_ref[...]).astype(BF16)
    for o_ref, (start, size, scale, slabs) in zip(out_refs, splits):
        y = _dot(h, w_ref[:, start:start + size])
        if scale != 1.0:
            y = y * scale
        if slabs:
            for j in range(size // LANES):
                o_ref[j] = y[:, j * LANES:(j + 1) * LANES].astype(o_ref.dtype)
        else:
            o_ref[...] = y.astype(o_ref.dtype)


def _norm_proj(x2d, g, w, splits, dtypes, tm):
    n, d = x2d.shape
    wn = w.shape[1]
    out_shape, out_specs = [], []
    for (_, size, _, slabs), dt in zip(splits, dtypes):
        if slabs:
            out_shape.append(jax.ShapeDtypeStruct((size // LANES, n, LANES), dt))
            out_specs.append(pl.BlockSpec((size // LANES, tm, LANES), lambda i: (0, i, 0)))
        else:
            out_shape.append(jax.ShapeDtypeStruct((n, size), dt))
            out_specs.append(pl.BlockSpec((tm, size), lambda i: (i, 0)))
    return pl.pallas_call(
        functools.partial(_norm_proj_kernel, splits=splits),
        grid=(n // tm,),
        in_specs=[pl.BlockSpec((tm, d), lambda i: (i, 0)),
                  pl.BlockSpec((1, d), lambda i: (0, 0)),
                  pl.BlockSpec((d, wn), lambda i: (0, 0))],
        out_specs=out_specs,
        out_shape=out_shape,
        compiler_params=_params(("parallel",)),
        name="norm_proj",
    )(x2d, g.reshape(1, d), w)


def _band_attn_kernel(slope_ref, m0_ref, l0_ref, q_ref, kc_ref, kp_ref, vc_ref, vp_ref, o_ref,
                      kcat, vcat, acc_ref, m_ref, l_ref, bias_ref, *, patterns, tt):
    hp = pl.program_id(1)
    first_tile = pl.program_id(2) == 0
    kcat[0:tt, :] = kp_ref[0]
    kcat[tt:, :] = kc_ref[0]
    vcat[0:tt, :] = vp_ref[0]
    vcat[tt:, :] = vc_ref[0]
    for e in range(2):
        m_ref[e] = jnp.full((tt, LANES), m0_ref[2 * hp + e], F32)
        l_ref[e] = jnp.full((tt, LANES), l0_ref[2 * hp + e], F32)
    acc_ref[...] = jnp.zeros(acc_ref.shape, F32)

    row = lax.broadcasted_iota(jnp.int32, (Q_BLOCK, 2 * Q_BLOCK), 0)
    col = lax.broadcasted_iota(jnp.int32, (Q_BLOCK, 2 * Q_BLOCK), 1)
    rel = row + Q_BLOCK - col
    lane = lax.broadcasted_iota(jnp.int32, (Q_BLOCK, LANES), 1)
    low = lane < HEAD_DIM
    ones = jnp.ones((2 * Q_BLOCK, LANES), BF16)

    for max_rel, d in patterns:
        band = (rel >= 0) & (rel <= max_rel)
        dist = (rel * d).astype(F32)
        for e in range(2):
            bias_ref[e] = slope_ref[2 * hp + e] * dist
        blocks_per_residue = tt // (Q_BLOCK * d)

        for idx in range(tt // Q_BLOCK):
            r = idx // blocks_per_residue
            t = idx % blocks_per_residue
            q0 = r + t * (Q_BLOCK * d)
            k0 = tt - Q_BLOCK * d + q0
            qrows = pl.ds(q0, Q_BLOCK, stride=d)
            krows = pl.ds(k0, 2 * Q_BLOCK, stride=d)
            valid = (band & ((col >= Q_BLOCK) | jnp.logical_not(first_tile))) if t == 0 else band
            qblk = q_ref[0, qrows, :].astype(BF16)
            kblk = kcat[krows, :].astype(BF16)
            vext = jnp.concatenate([vcat[krows, :].astype(BF16), ones], axis=1)
            alphas, pvs = [], []
            for e in range(2):
                qm = jnp.where(low if e == 0 else jnp.logical_not(low), qblk, jnp.zeros_like(qblk))
                s = _dot_nt(qm, kblk)
                s = jnp.where(valid, s - bias_ref[e], NEG_INF)
                m_old = m_ref[e, qrows, :]
                m_new = jnp.maximum(m_old, jnp.max(s, axis=-1, keepdims=True))
                alpha = jnp.exp2(m_old - m_new)
                p = jnp.exp2(s - jnp.concatenate([m_new, m_new], axis=1))
                pv = _dot(p.astype(BF16), vext)
                m_ref[e, qrows, :] = m_new
                l_ref[e, qrows, :] = alpha * l_ref[e, qrows, :] + pv[:, LANES:]
                alphas.append(alpha)
                pvs.append(pv[:, :LANES])
            acc_ref[qrows, :] = (jnp.where(low, alphas[0], alphas[1]) * acc_ref[qrows, :]
                                 + jnp.where(low, pvs[0], pvs[1]))

    lane_t = lax.broadcasted_iota(jnp.int32, (tt, LANES), 1)
    o_ref[...] = (acc_ref[...] / jnp.where(lane_t < HEAD_DIM, l_ref[0], l_ref[1])).astype(o_ref.dtype)


def _band_attn(q, k, v, *, batch, patterns, slopes, m0, l0):
    n_pairs, n, _ = q.shape
    n_kv = k.shape[0]
    seq = n // batch
    max_d = max(d for _, d in patterns)
    tt = min(seq, max(2048, Q_BLOCK * max_d))
    tiles = seq // tt
    cur = lambda b, p, i: (p, b * tiles + i, 0)
    cur_kv = lambda b, p, i: (p * n_kv // n_pairs, b * tiles + i, 0)
    prev_kv = lambda b, p, i: (p * n_kv // n_pairs, b * tiles + jnp.maximum(i - 1, 0), 0)
    smem = pl.BlockSpec(memory_space=pltpu.SMEM)
    blk = lambda im: pl.BlockSpec((1, tt, LANES), im)
    stat = pltpu.VMEM((2, tt, LANES), F32)
    return pl.pallas_call(
        functools.partial(_band_attn_kernel, patterns=patterns, tt=tt),
        grid=(batch, n_pairs, tiles),
        in_specs=[smem, smem, smem, blk(cur), blk(cur_kv), blk(prev_kv), blk(cur_kv), blk(prev_kv)],
        out_specs=pl.BlockSpec((tt, LANES), lambda b, p, i: (b * tiles + i, p)),
        out_shape=jax.ShapeDtypeStruct((n, n_pairs * LANES), BF16),
        scratch_shapes=[pltpu.VMEM((2 * tt, LANES), F32), pltpu.VMEM((2 * tt, LANES), F32),
                        pltpu.VMEM((tt, LANES), F32), stat, stat,
                        pltpu.VMEM((2, Q_BLOCK, 2 * Q_BLOCK), F32)],
        compiler_params=_params(("parallel", "parallel", "arbitrary")),
        name="band_attn_p%d" % len(patterns),
    )(slopes, m0, l0, q, k, k, v, v)


def _gla_kernel(q_ref, k_ref, v_ref, r_ref, g_ref, wg_ref, bg_ref, ng_ref, o_ref, st_ref, *,
                chunks_per_step):
    c = B_CHUNK

    @pl.when(pl.program_id(0) == 0)
    def _():
        st_ref[...] = jnp.zeros_like(st_ref)

    batch = q_ref.shape[0]
    log_as = []
    for b in range(batch):
        gate = _dot(g_ref[b].astype(BF16), wg_ref[...]) + bg_ref[...]
        log_as.append(jax.nn.log_sigmoid(gate) / B_GATE_TAU)
    ri = lax.broadcasted_iota(jnp.int32, (c, c), 0)
    ci = lax.broadcasted_iota(jnp.int32, (c, c), 1)
    tri = (ri >= ci).astype(BF16)
    lane = lax.broadcasted_iota(jnp.int32, (c, LANES), 1)
    low = lane < B_DK
    ri2 = lax.broadcasted_iota(jnp.int32, (2 * c, 2 * c), 0)
    ci2 = lax.broadcasted_iota(jnp.int32, (2 * c, 2 * c), 1)
    causal2 = ((ri2 < c) == (ci2 < c)) & ((ri2 % c) >= (ci2 % c))
    low_sq = ci2 < B_DK
    norm_g = ng_ref[...]
    n_pairs = B_HEADS // 2
    dk2 = B_HEADS * B_DK

    def stack_heads(t):
        zeros = jnp.zeros_like(t)
        return jnp.concatenate([jnp.where(low, t, zeros), jnp.where(low, zeros, t)], axis=0)

    for ch, b in [(ch, b) for ch in range(chunks_per_step) for b in range(batch)]:
        rows = slice(ch * c, (ch + 1) * c)
        la = log_as[b][rows]
        hi = la.astype(BF16)
        rem = la - hi.astype(F32)
        mid = rem.astype(BF16)
        lo = (rem - mid.astype(F32)).astype(BF16)
        terms = _dot(tri, jnp.concatenate([hi, mid, lo], axis=1))
        bcum = terms[:, :dk2] + terms[:, dk2:2 * dk2] + terms[:, 2 * dk2:]
        b_last = bcum[c - 1:c, :]
        b_mid = bcum[c // 2 - 1:c // 2, :]
        qf = q_ref[b, rows, :] * (B_DK ** -0.5)
        kf = k_ref[b, rows, :]
        q_in = (qf * jnp.exp(bcum - b_mid)).astype(BF16)
        k_in = (kf * jnp.exp(b_mid - bcum)).astype(BF16)
        q_st = (qf * jnp.exp(bcum)).astype(BF16)
        k_st = (kf * jnp.exp(b_last - bcum)).astype(BF16)
        decay = jnp.exp(b_last)
        for pr in range(n_pairs):
            pc = slice(pr * LANES, (pr + 1) * LANES)
            st = st_ref[b, pr]
            vcols = slice(2 * pr * B_DV, (2 * pr + 2) * B_DV)
            v_lanes = v_ref[b, rows, vcols].astype(BF16)
            v_rows = jnp.concatenate([v_lanes[:, :B_DV], v_lanes[:, B_DV:]], axis=0)
            att = _dot_nt(stack_heads(q_in[:, pc]), stack_heads(k_in[:, pc]))
            att = jnp.where(causal2, att, 0.0)
            o = _dot(att.astype(BF16), v_rows) + _dot_nt(stack_heads(q_st[:, pc]), st.astype(BF16))
            o = _rms(o, norm_g)
            rg = r_ref[b, rows, vcols]
            gate_r = jax.nn.silu(jnp.concatenate([rg[:, :B_DV], rg[:, B_DV:]], axis=0))
            o = (o * gate_r).astype(o_ref.dtype)
            o_ref[b, rows, 2 * pr * B_DV:(2 * pr + 1) * B_DV] = o[:c]
            o_ref[b, rows, (2 * pr + 1) * B_DV:(2 * pr + 2) * B_DV] = o[c:]
            d_st = _dot_tn(v_lanes, k_st[:, pc])
            st_ref[b, pr] = st * decay[:, pc] + jnp.where(low_sq, d_st[:B_DV], d_st[B_DV:])


def _gla(qb, kb, vb, rb, gb, w_gate, b_gate, norm_g, batch):
    n = qb.shape[0]
    seq = n // batch
    t = min(256, seq)
    r3 = lambda a: a.reshape(batch, seq, a.shape[1])
    blk = lambda c: pl.BlockSpec((batch, t, c), lambda i: (0, i, 0))
    full = lambda a: pl.BlockSpec(a.shape, lambda i: (0,) * a.ndim)
    dk2 = B_HEADS * B_DK
    dv2 = B_HEADS * B_DV
    out = pl.pallas_call(
        functools.partial(_gla_kernel, chunks_per_step=t // B_CHUNK),
        grid=(seq // t,),
        in_specs=[blk(dk2), blk(dk2), blk(dv2), blk(dv2), blk(LANES),
                  full(w_gate), full(b_gate), full(norm_g)],
        out_specs=blk(dv2),
        out_shape=jax.ShapeDtypeStruct((batch, seq, dv2), BF16),
        scratch_shapes=[pltpu.VMEM((batch, B_HEADS // 2, B_DV, 2 * B_DK), F32)],
        compiler_params=_params(("arbitrary",)),
        name="gla",
    )(r3(qb), r3(kb), r3(vb), r3(rb), r3(gb), w_gate, b_gate, norm_g)
    return out.reshape(n, dv2)


def _proj_ffn_kernel(a_ref, b_ref, wa_ref, wb_ref, x_ref, gmix_ref, gpre_ref, wg_ref, wu_ref, wd_ref,
                     gffn_ref, x_out, *, tf):
    y = _dot(a_ref[...], wa_ref[...]) + _dot(b_ref[...], wb_ref[...])
    x1 = x_ref[...] + _rms(y, gmix_ref[...])
    h = _rms(x1, gpre_ref[...]).astype(BF16)
    f = None
    for c in range(wg_ref.shape[1] // tf):
        cols = slice(c * tf, (c + 1) * tf)
        g = _dot(h, wg_ref[:, cols])
        u = _dot(h, wu_ref[:, cols])
        part = _dot((jax.nn.silu(g) * u).astype(BF16), wd_ref[cols, :])
        f = part if f is None else f + part
    x_out[...] = x1 + _rms(f, gffn_ref[...])


def _proj_ffn(a, b, wa, wb, x2d, g_mix, g_pre, wg, wu, wd, g_ffn, tm, tf):
    n, d = x2d.shape
    ca, cb = a.shape[1], b.shape[1]
    dff = wg.shape[1]
    row = lambda c: pl.BlockSpec((tm, c), lambda i: (i, 0))
    resident = lambda r, c: pl.BlockSpec((r, c), lambda i: (0, 0), pipeline_mode=pl.Buffered(1))
    vec = lambda g: g.reshape(1, d)
    return pl.pallas_call(
        functools.partial(_proj_ffn_kernel, tf=tf),
        grid=(n // tm,),
        in_specs=[row(ca), row(cb), resident(ca, d), resident(cb, d), row(d), resident(1, d),
                  resident(1, d), resident(d, dff), resident(d, dff), resident(dff, d), resident(1, d)],
        out_specs=row(d),
        out_shape=jax.ShapeDtypeStruct((n, d), F32),
        compiler_params=_params(("parallel",)),
        name="proj_ffn",
    )(a, b, wa, wb, x2d, vec(g_mix), vec(g_pre), wg, wu, wd, vec(g_ffn))


def _mla_prep_kernel(cq_ref, ckv_ref, kr_ref, qn_ref, kvn_ref, wq_ref, wk_ref, wv_ref,
                     cosq_ref, sinq_ref, cosk_ref, sink_ref, q_out, k_out, v_out, *, scale):
    hq = _rms(cq_ref[...], qn_ref[...]).astype(BF16)
    hkv = _rms(ckv_ref[...], kvn_ref[...]).astype(BF16)
    lane = lax.broadcasted_iota(jnp.int32, (cq_ref.shape[0], LANES), 1)
    half = D_ROPE // 2

    def swap_halves(t, start):
        return jnp.where(lane < start + half, pltpu.roll(t, LANES - half, 1), pltpu.roll(t, half, 1))

    kr = kr_ref[...]
    kpe = kr * cosk_ref[...] + swap_halves(kr, 0) * sink_ref[...]
    kpe = pltpu.roll(kpe, D_NOPE, 1)
    cosq = cosq_ref[...]
    sinq = sinq_ref[...]
    for h in range(D_HEADS):
        cols = slice(h * LANES, (h + 1) * LANES)
        q = _dot(hq, wq_ref[:, cols])
        q = q * cosq + swap_halves(q, D_NOPE) * sinq
        q_out[:, cols] = (q * scale).astype(q_out.dtype)
        k = _dot(hkv, wk_ref[:, cols])
        k_out[:, cols] = (k + kpe).astype(k_out.dtype)
    for pr in range(D_HEADS // 2):
        v = _dot(hkv, wv_ref[:, pr * LANES:(pr + 1) * LANES])
        v_out[:, 2 * pr * LANES:(2 * pr + 1) * LANES] = v.astype(v_out.dtype)
        v_out[:, (2 * pr + 1) * LANES:(2 * pr + 2) * LANES] = jnp.ones(v.shape, v_out.dtype)


def _mla_prep(c_q, c_kv, kr, q_norm, kv_norm, wq, wk, wv, tables, seq, tm):
    n = c_q.shape[0]
    tiles_per_seq = seq // tm
    row = lambda c: pl.BlockSpec((tm, c), lambda i: (i, 0))
    const = lambda a: pl.BlockSpec(a.shape, lambda i: (0,) * a.ndim)
    tab = pl.BlockSpec((tm, LANES), lambda i: (i % tiles_per_seq, 0))
    hw = D_HEADS * LANES
    return pl.pallas_call(
        functools.partial(_mla_prep_kernel, scale=float((D_NOPE + D_ROPE) ** -0.5) * LOG2E),
        grid=(n // tm,),
        in_specs=[row(D_Q_RANK), row(D_KV_RANK), row(LANES), const(q_norm), const(kv_norm),
                  const(wq), const(wk), const(wv), tab, tab, tab, tab],
        out_specs=[row(hw), row(hw), row(hw)],
        out_shape=[jax.ShapeDtypeStruct((n, hw), BF16)] * 3,
        compiler_params=_params(("parallel",)),
        name="mla_prep",
    )(c_q, c_kv, kr, q_norm, kv_norm, wq, wk, wv, *tables)


def _rope_tables(seq):
    half = D_ROPE // 2
    freqs = np.float32(ROPE_BASE) ** (-np.arange(half, dtype=np.float32) / np.float32(half))
    ang = np.arange(seq, dtype=np.float32)[:, None] * freqs[None, :]
    cos, sin = jnp.asarray(np.cos(ang), F32), jnp.asarray(np.sin(ang), F32)
    ones = jnp.ones((seq, D_NOPE), F32)
    zeros = lambda c: jnp.zeros((seq, c), F32)
    pad = LANES - D_NOPE - D_ROPE
    cosq = jnp.concatenate([ones, cos, cos, zeros(pad)], axis=1)
    sinq = jnp.concatenate([zeros(D_NOPE), -sin, sin, zeros(pad)], axis=1)
    cosk = jnp.concatenate([cos, cos, zeros(LANES - D_ROPE)], axis=1)
    sink = jnp.concatenate([-sin, sin, zeros(LANES - D_ROPE)], axis=1)
    return cosq, sinq, cosk, sink


def _mla_attn_kernel(qe_ref, qo_ref, k_ref, v_ref, oe_ref, oo_ref, q_sc, s_ref, p_ref, m_ref, l_ref,
                     a_ref, acc_ref, *, tq, rb, n_full):
    tk = 2 * tq
    t_even = pl.program_id(2)
    q_sc[0] = qe_ref[0]
    q_sc[1] = qo_ref[0]
    m_ref[...] = jnp.full(m_ref.shape, NEG_INF, F32)
    l_ref[...] = jnp.zeros(l_ref.shape, F32)
    acc_ref[...] = jnp.zeros(acc_ref.shape, F32)

    diag_odd = (1, n_full - t_even, tk, "odd")
    diag_even = (0, t_even, tq, "even")

    def full_tile(u):
        sel = (u >= t_even).astype(jnp.int32)
        return (sel, u - sel * t_even, tk, None)

    def scores(tile, buf):
        sel, kt, width, _ = tile
        k0 = pl.multiple_of(kt * tk, tk)
        for e in range(2):
            qblk = q_sc[sel, :, e * LANES:(e + 1) * LANES]
            kblk = k_ref[0, pl.ds(k0, width), e * LANES:(e + 1) * LANES]
            s_ref[buf, e, :, 0:width] = _dot_nt(qblk, kblk)

    def softmax(tile, buf):
        sel, _, width, mask = tile
        for e in range(2):
            for r in range(tq // rb):
                rows = slice(r * rb, (r + 1) * rb)
                s = s_ref[buf, e, rows, 0:width]
                if mask is not None:
                    row = r * rb + lax.broadcasted_iota(jnp.int32, (rb, width), 0)
                    col = lax.broadcasted_iota(jnp.int32, (rb, width), 1)
                    s = jnp.where(row >= (col if mask == "even" else col - tq), s, NEG_INF)
                m_old = m_ref[sel, e, rows, :]
                m_new = jnp.maximum(m_old, jnp.max(s, axis=-1, keepdims=True))
                alpha = jnp.exp2(m_old - m_new)
                p = jnp.exp2(s - jnp.concatenate([m_new] * (width // LANES), axis=1))
                m_ref[sel, e, rows, :] = m_new
                a_ref[buf, e, rows, :] = alpha
                p_ref[buf, e, rows, 0:width] = p.astype(BF16)

    def values(tile, buf):
        sel, kt, width, _ = tile
        k0 = pl.multiple_of(kt * tk, tk)
        vblk = v_ref[0, pl.ds(k0, width), :]
        for e in range(2):
            pv = _dot(p_ref[buf, e, :, 0:width], vblk)
            alpha = a_ref[buf, e]
            acc_ref[sel, e] = alpha * acc_ref[sel, e] + pv[:, :LANES]
            l_ref[sel, e] = alpha * l_ref[sel, e] + pv[:, LANES:]

    def step(t, prev, this, nxt):
        if nxt is not None:
            scores(nxt, (t + 1) % 2)
        if prev is not None:
            values(prev, (t - 1) % 2)
        softmax(this, t % 2)

    def static_tile(t):
        return (diag_odd, diag_even)[t] if t < 2 else full_tile(jnp.int32(t - 2))

    n_pos = n_full + 2
    loop_lo = min(3, n_pos)
    loop_pairs = max(0, (n_pos - 1 - loop_lo) // 2)
    loop_hi = loop_lo + 2 * loop_pairs

    scores(diag_odd, 0)
    for t in range(loop_lo):
        step(t, static_tile(t - 1) if t >= 1 else None, static_tile(t),
             static_tile(t + 1) if t + 1 < n_pos else None)

    def body(jj, carry):
        for t_off in range(2):
            t = loop_lo + t_off
            u = 2 * jj + (loop_lo - 2) + t_off
            step(t, full_tile(u - 1), full_tile(u), full_tile(u + 1))
        return carry

    lax.fori_loop(0, loop_pairs, body, 0)
    for t in range(loop_hi, n_pos):
        step(t, static_tile(t - 1), static_tile(t), static_tile(t + 1) if t + 1 < n_pos else None)
    values(static_tile(n_pos - 1), (n_pos - 1) % 2)

    lane = lax.broadcasted_iota(jnp.int32, (tq, LANES), 1)
    for sel, o_ref in ((0, oe_ref), (1, oo_ref)):
        outs = [acc_ref[sel, e] / l_ref[sel, e] for e in range(2)]
        o_ref[0] = jnp.where(lane < D_V, outs[0], outs[1]).astype(o_ref.dtype)


def _mla_attn(q, k, v, batch, tq):
    n = q.shape[0]
    seq = n // batch
    n_pairs = D_HEADS // 2
    half = seq // tq // 2
    q3 = q.reshape(batch, seq, q.shape[1])
    k3 = k.reshape(batch, seq, k.shape[1])
    v3 = v.reshape(batch, seq, v.shape[1])
    stat = pltpu.VMEM((2, 2, tq, LANES), F32)
    q_spec = lambda im: pl.BlockSpec((1, tq, 2 * LANES), im)
    kv_map = lambda b, p, i: (b, 0, p)
    k_spec = pl.BlockSpec((1, seq, 2 * LANES), kv_map)
    v_spec = pl.BlockSpec((1, seq, 2 * LANES), kv_map, pipeline_mode=pl.Buffered(1))
    o_shape = jax.ShapeDtypeStruct((batch, half * tq, n_pairs * LANES), BF16)
    o_even, o_odd = pl.pallas_call(
        functools.partial(_mla_attn_kernel, tq=tq, rb=16, n_full=half - 1),
        grid=(batch, n_pairs, half),
        in_specs=[q_spec(lambda b, p, i: (b, 2 * i, p)),
                  q_spec(lambda b, p, i: (b, 2 * half - 1 - 2 * i, p)),
                  k_spec, v_spec],
        out_specs=[pl.BlockSpec((1, tq, LANES), lambda b, p, i: (b, i, p)),
                   pl.BlockSpec((1, tq, LANES), lambda b, p, i: (b, half - 1 - i, p))],
        out_shape=[o_shape, o_shape],
        scratch_shapes=[pltpu.VMEM((2, tq, 2 * LANES), BF16),
                        pltpu.VMEM((2, 2, tq, 2 * tq), F32), pltpu.VMEM((2, 2, tq, 2 * tq), BF16),
                        stat, stat, stat, stat],
        compiler_params=_params(("parallel", "parallel", "arbitrary")),
        name="mla_attn",
    )(q3, q3, k3, v3)
    shape4 = (batch, half, tq, n_pairs * LANES)
    out = jnp.stack([o_even.reshape(shape4), o_odd.reshape(shape4)], axis=2)
    return out.reshape(n, n_pairs * LANES)


def _pad_cols(w, width):
    return jnp.pad(w, ((0, 0), (0, width - w.shape[1])))


def _layer0_w_in(w):
    return _pad_cols(w, 25 * LANES).astype(BF16)


def _layer1_w_in(w):
    hd = HEAD_DIM
    qc = w[:, :C_HEADS * hd]
    o = C_HEADS * hd
    kc = w[:, o:o + C_KV_HEADS * hd]
    o += C_KV_HEADS * hd
    vc = w[:, o:o + C_KV_HEADS * hd]
    o += C_KV_HEADS * hd
    rest = w[:, o:o + D_Q_RANK + D_KV_RANK]
    o += D_Q_RANK + D_KV_RANK
    kr = _pad_cols(w[:, o:o + D_ROPE], LANES)
    rep = lambda t: jnp.concatenate([t[:, g * hd:(g + 1) * hd] for g in range(C_KV_HEADS) for _ in range(2)], axis=1)
    return jnp.concatenate([qc, rep(kc), rep(vc), rest, kr], axis=1).astype(BF16)


def _mla_weights(w_uq, w_ukv):
    dq = D_NOPE + D_ROPE
    wq = w_uq.reshape(D_Q_RANK, D_HEADS, dq)
    wq = jnp.pad(wq, ((0, 0), (0, 0), (0, LANES - dq))).reshape(D_Q_RANK, D_HEADS * LANES)
    wkv = w_ukv.reshape(D_KV_RANK, D_HEADS, D_NOPE + D_V)
    wk = jnp.pad(wkv[:, :, :D_NOPE], ((0, 0), (0, 0), (0, LANES - D_NOPE)))
    wk = wk.reshape(D_KV_RANK, D_HEADS * LANES)
    wv = wkv[:, :, D_NOPE:].reshape(D_KV_RANK, D_HEADS * D_V)
    return wq.astype(BF16), wk.astype(BF16), wv.astype(BF16)


def _row_tile(n, want):
    return min(want, n)


def _splits(sizes, scales, slabs):
    starts = np.concatenate([[0], np.cumsum(sizes)[:-1]])
    return tuple((int(s), int(z), float(c), bool(b)) for s, z, c, b in zip(starts, sizes, scales, slabs))


def _mix_even(x2d, batch, g_pre, w_in, gla_w_gate_up, gla_b_gate, gla_norm):
    n = x2d.shape[0]
    ah = A_HEADS * HEAD_DIM
    dk2, dv2 = B_HEADS * B_DK, B_HEADS * B_DV
    sizes = (ah, ah, ah, dk2, dk2, dv2, dv2, LANES)
    scales = (HEAD_DIM ** -0.5 * LOG2E,) + (1.0,) * 7
    slabs = (True, True, True) + (False,) * 5
    qa, ka, va, qb, kb, vb, rb, gb = _norm_proj(x2d, g_pre, _layer0_w_in(w_in),
                                                _splits(sizes, scales, slabs), (F32,) * 8,
                                                _row_tile(n, 512))
    o_a = _band_attn(qa, ka, va, batch=batch,
                     patterns=tuple((w // d, d) for w, d in A_PATTERNS),
                     slopes=jnp.asarray(_alibi_slopes(A_HEADS) * np.float32(LOG2E)),
                     m0=jnp.full((A_HEADS,), NEG_INF, F32), l0=jnp.zeros((A_HEADS,), F32))
    w_gate = jnp.pad(gla_w_gate_up, ((0, LANES - B_GATE_RANK), (0, 0))).astype(BF16)
    o_b = _gla(qb, kb, vb, rb, gb, w_gate, gla_b_gate.reshape(1, dk2), gla_norm.reshape(1, B_DV), batch)
    return o_a, o_b


def _mix_odd(x2d, batch, g_pre, w_in, sinks, q_norm, w_uq, kv_norm, w_ukv):
    n = x2d.shape[0]
    seq = n // batch
    ch = C_HEADS * HEAD_DIM
    ckv = 2 * C_KV_HEADS * HEAD_DIM
    sizes = (ch, ckv, ckv, D_Q_RANK, D_KV_RANK, LANES)
    scales = (HEAD_DIM ** -0.5 * LOG2E,) + (1.0,) * 5
    slabs = (True, True, True, False, False, False)
    qc, kc, vc, c_q, c_kv, kr = _norm_proj(x2d, g_pre, _layer1_w_in(w_in),
                                           _splits(sizes, scales, slabs), (F32,) * 6,
                                           _row_tile(n, 512))
    o_c = _band_attn(qc, kc, vc, batch=batch, patterns=((C_WINDOW - 1, 1),),
                     slopes=jnp.asarray(_alibi_slopes(C_HEADS) * np.float32(LOG2E)),
                     m0=sinks.astype(F32) * LOG2E, l0=jnp.ones((C_HEADS,), F32))
    wq, wk, wv = _mla_weights(w_uq, w_ukv)
    q, k, v = _mla_prep(c_q, c_kv, kr, q_norm.reshape(1, D_Q_RANK), kv_norm.reshape(1, D_KV_RANK),
                        wq, wk, wv, _rope_tables(seq), seq, _row_tile(seq, 512))
    o_d = _mla_attn(q, k, v, batch, _row_tile(seq // 2, 512))
    return o_c, o_d


def kernel(x, norm_mix_pre, norm_mix_post, norm_ffn_pre, norm_ffn_post, ffn_w_gate, ffn_w_up, ffn_w_down, ab_w_in, ab_w_out, gla_w_gate_up, gla_b_gate, gla_norm, cd_w_in, cd_w_out, swa_sinks, mla_q_norm, mla_w_uq, mla_kv_norm, mla_w_ukv):
    batch, seq, d = x.shape
    n = batch * seq
    x2d = x.reshape(n, d)
    depth = norm_mix_pre.shape[0]
    for layer in range(depth):
        i = layer // 2
        if layer % 2 == 0:
            o1, o2 = _mix_even(x2d, batch, norm_mix_pre[layer], ab_w_in[i], gla_w_gate_up[i],
                               gla_b_gate[i], gla_norm[i])
            w_out = ab_w_out[i].astype(BF16)
        else:
            o1, o2 = _mix_odd(x2d, batch, norm_mix_pre[layer], cd_w_in[i], swa_sinks[i],
                              mla_q_norm[i], mla_w_uq[i], mla_kv_norm[i], mla_w_ukv[i])
            w_out = cd_w_out[i].astype(BF16)
        c1 = o1.shape[1]
        x2d = _proj_ffn(o1, o2, w_out[:c1], w_out[c1:], x2d, norm_mix_post[layer],
                        norm_ffn_pre[layer], ffn_w_gate[layer].astype(BF16),
                        ffn_w_up[layer].astype(BF16), ffn_w_down[layer].astype(BF16),
                        norm_ffn_post[layer], _row_tile(n, 512), 256)
    return x2d.reshape(batch, seq, d)
```

```python
import functools

import numpy as np
import jax
import jax.numpy as jnp
from jax import lax
from jax.experimental import pallas as pl
from jax.experimental.pallas import tpu as pltpu

F32 = jnp.float32
BF16 = jnp.bfloat16

LANES = 128
V7X_VMEM_BYTES = 64 * 1024 * 1024
VMEM_LIMIT = 56 * 1024 * 1024
LOG2E = float(np.log2(np.e))

D_MODEL = 1024
HEAD_DIM = 64
Q_BLOCK = 128
EPS = 1e-6
NEG_INF = -1e30
A_HEADS = 8
A_PATTERNS = ((128, 1), (512, 4), (2048, 16))
B_HEADS = 4
B_DK = 64
B_DV = 128
B_GATE_RANK = 16
B_GATE_TAU = 16.0
B_CHUNK = 64
C_HEADS = 8
C_KV_HEADS = 2
C_WINDOW = 128
D_HEADS = 8
D_Q_RANK = 384
D_KV_RANK = 256
D_NOPE = 64
D_ROPE = 32
D_V = 64
ROPE_BASE = 10000.0
D_FF = 2816


def _alibi_slopes(n):
    return np.array([2.0 ** (-8.0 * (i + 1) / n) for i in range(n)], dtype=np.float32)


def _params(semantics):
    return pltpu.CompilerParams(dimension_semantics=semantics, vmem_limit_bytes=VMEM_LIMIT)


def _rms(x, g):
    return x * lax.rsqrt(jnp.mean(x * x, axis=-1, keepdims=True) + EPS) * g


def _dot(a, b):
    return jnp.dot(a, b, preferred_element_type=F32)


def _dot_nt(a, b):
    return lax.dot_general(a, b, (((1,), (1,)), ((), ())), preferred_element_type=F32)


def _dot_tn(a, b):
    return lax.dot_general(a, b, (((0,), (0,)), ((), ())), preferred_element_type=F32)


def _norm_proj_kernel(x_ref, g_ref, w_ref, *out_refs, splits):
    h = _rms(x_ref[...], g_ref[...]).astype(BF16)
    for o_ref, (start, size, scale, slabs) in zip(out_refs, splits):
        y = _dot(h, w_ref[:, start:start + size])
        if scale != 1.0:
            y = y * scale
        if slabs:
            for j in range(size // LANES):
                o_ref[j] = y[:, j * LANES:(j + 1) * LANES].astype(o_ref.dtype)
        else:
            o_ref[...] = y.astype(o_ref.dtype)


def _norm_proj(x2d, g, w, splits, dtypes, tm):
    n, d = x2d.shape
    wn = w.shape[1]
    out_shape, out_specs = [], []
    for (_, size, _, slabs), dt in zip(splits, dtypes):
        if slabs:
            out_shape.append(jax.ShapeDtypeStruct((size // LANES, n, LANES), dt))
            out_specs.append(pl.BlockSpec((size // LANES, tm, LANES), lambda i: (0, i, 0)))
        else:
            out_shape.append(jax.ShapeDtypeStruct((n, size), dt))
            out_specs.append(pl.BlockSpec((tm, size), lambda i: (i, 0)))
    return pl.pallas_call(
        functools.partial(_norm_proj_kernel, splits=splits),
        grid=(n // tm,),
        in_specs=[pl.BlockSpec((tm, d), lambda i: (i, 0)),
                  pl.BlockSpec((1, d), lambda i: (0, 0)),
                  pl.BlockSpec((d, wn), lambda i: (0, 0))],
        out_specs=out_specs,
        out_shape=out_shape,
        compiler_params=_params(("parallel",)),
        name="norm_proj",
    )(x2d, g.reshape(1, d), w)


def _band_attn_kernel(slope_ref, m0_ref, l0_ref, q_ref, kc_ref, kp_ref, vc_ref, vp_ref, o_ref,
                      kcat, vcat, acc_ref, m_ref, l_ref, bias_ref, *, patterns, tt):
    hp = pl.program_id(1)
    first_tile = pl.program_id(2) == 0
    kcat[0:tt, :] = kp_ref[0]
    kcat[tt:, :] = kc_ref[0]
    vcat[0:tt, :] = vp_ref[0]
    vcat[tt:, :] = vc_ref[0]
    m_init = [jnp.full((Q_BLOCK, LANES), m0_ref[2 * hp + e], F32) for e in range(2)]
    l_init = [jnp.full((Q_BLOCK, LANES), l0_ref[2 * hp + e], F32) for e in range(2)]

    row = lax.broadcasted_iota(jnp.int32, (Q_BLOCK, 2 * Q_BLOCK), 0)
    col = lax.broadcasted_iota(jnp.int32, (Q_BLOCK, 2 * Q_BLOCK), 1)
    rel = row + Q_BLOCK - col
    lane = lax.broadcasted_iota(jnp.int32, (Q_BLOCK, LANES), 1)
    low = lane < HEAD_DIM
    ones = jnp.ones((2 * Q_BLOCK, LANES), BF16)

    for pi, (max_rel, d) in enumerate(patterns):
        first = pi == 0
        fused_out = pi == len(patterns) - 1 and d == 1
        band = (rel >= 0) & (rel <= max_rel)
        dist = (rel * d).astype(F32)
        for e in range(2):
            bias_ref[e] = slope_ref[2 * hp + e] * dist
        blocks_per_residue = tt // (Q_BLOCK * d)

        for idx in range(tt // Q_BLOCK):
            r = idx // blocks_per_residue
            t = idx % blocks_per_residue
            q0 = r + t * (Q_BLOCK * d)
            k0 = tt - Q_BLOCK * d + q0
            qrows = pl.ds(q0, Q_BLOCK, stride=d)
            krows = pl.ds(k0, 2 * Q_BLOCK, stride=d)
            valid = (band & ((col >= Q_BLOCK) | jnp.logical_not(first_tile))) if t == 0 else band
            qblk = q_ref[0, qrows, :].astype(BF16)
            kblk = kcat[krows, :].astype(BF16)
            vext = jnp.concatenate([vcat[krows, :].astype(BF16), ones], axis=1)
            alphas, pvs, ls = [], [], []
            for e in range(2):
                qm = jnp.where(low if e == 0 else jnp.logical_not(low), qblk, jnp.zeros_like(qblk))
                s = _dot_nt(qm, kblk)
                s = jnp.where(valid, s - bias_ref[e], NEG_INF)
                m_old = m_init[e] if first else m_ref[e, qrows, :]
                l_old = l_init[e] if first else l_ref[e, qrows, :]
                m_new = jnp.maximum(m_old, jnp.max(s, axis=-1, keepdims=True))
                alpha = jnp.exp2(m_old - m_new)
                p = jnp.exp2(s - jnp.concatenate([m_new, m_new], axis=1))
                pv = _dot(p.astype(BF16), vext)
                l_new = alpha * l_old + pv[:, LANES:]
                if not fused_out:
                    m_ref[e, qrows, :] = m_new
                    l_ref[e, qrows, :] = l_new
                alphas.append(alpha)
                pvs.append(pv[:, :LANES])
                ls.append(l_new)
            acc = jnp.where(low, pvs[0], pvs[1])
            if not first:
                acc = jnp.where(low, alphas[0], alphas[1]) * acc_ref[qrows, :] + acc
            if fused_out:
                o_ref[pl.ds(q0, Q_BLOCK), :] = (acc / jnp.where(low, ls[0], ls[1])).astype(o_ref.dtype)
            else:
                acc_ref[qrows, :] = acc

    if patterns[-1][1] != 1:
        lane_t = lax.broadcasted_iota(jnp.int32, (tt, LANES), 1)
        o_ref[...] = (acc_ref[...] / jnp.where(lane_t < HEAD_DIM, l_ref[0], l_ref[1])).astype(o_ref.dtype)


def _band_attn(q, k, v, *, batch, patterns, slopes, m0, l0):
    n_pairs, n, _ = q.shape
    n_kv = k.shape[0]
    seq = n // batch
    max_d = max(d for _, d in patterns)
    tt = min(seq, max(2048, Q_BLOCK * max_d))
    tiles = seq // tt
    cur = lambda b, p, i: (p, b * tiles + i, 0)
    cur_kv = lambda b, p, i: (p * n_kv // n_pairs, b * tiles + i, 0)
    prev_kv = lambda b, p, i: (p * n_kv // n_pairs, b * tiles + jnp.maximum(i - 1, 0), 0)
    smem = pl.BlockSpec(memory_space=pltpu.SMEM)
    blk = lambda im: pl.BlockSpec((1, tt, LANES), im)
    stat = pltpu.VMEM((2, tt, LANES), F32)
    return pl.pallas_call(
        functools.partial(_band_attn_kernel, patterns=patterns, tt=tt),
        grid=(batch, n_pairs, tiles),
        in_specs=[smem, smem, smem, blk(cur), blk(cur_kv), blk(prev_kv), blk(cur_kv), blk(prev_kv)],
        out_specs=pl.BlockSpec((tt, LANES), lambda b, p, i: (b * tiles + i, p)),
        out_shape=jax.ShapeDtypeStruct((n, n_pairs * LANES), BF16),
        scratch_shapes=[pltpu.VMEM((2 * tt, LANES), F32), pltpu.VMEM((2 * tt, LANES), F32),
                        pltpu.VMEM((tt, LANES), F32), stat, stat,
                        pltpu.VMEM((2, Q_BLOCK, 2 * Q_BLOCK), F32)],
        compiler_params=_params(("parallel", "parallel", "arbitrary")),
        name="band_attn_p%d" % len(patterns),
    )(slopes, m0, l0, q, k, k, v, v)


def _gla_kernel(q_ref, k_ref, v_ref, r_ref, g_ref, wg_ref, bg_ref, ng_ref, o_ref, st_ref, *,
                chunks_per_step):
    c = B_CHUNK

    @pl.when(pl.program_id(0) == 0)
    def _():
        st_ref[...] = jnp.zeros_like(st_ref)

    batch = q_ref.shape[0]
    log_as = []
    for b in range(batch):
        gate = _dot(g_ref[b].astype(BF16), wg_ref[...]) + bg_ref[...]
        log_as.append(jax.nn.log_sigmoid(gate) / B_GATE_TAU)
    ri = lax.broadcasted_iota(jnp.int32, (c, c), 0)
    ci = lax.broadcasted_iota(jnp.int32, (c, c), 1)
    tri = (ri >= ci).astype(BF16)
    lane = lax.broadcasted_iota(jnp.int32, (c, LANES), 1)
    low = lane < B_DK
    ri2 = lax.broadcasted_iota(jnp.int32, (2 * c, 2 * c), 0)
    ci2 = lax.broadcasted_iota(jnp.int32, (2 * c, 2 * c), 1)
    causal2 = ((ri2 < c) == (ci2 < c)) & ((ri2 % c) >= (ci2 % c))
    low_sq = ci2 < B_DK
    norm_g = ng_ref[...]
    n_pairs = B_HEADS // 2
    dk2 = B_HEADS * B_DK

    def stack_heads(t):
        zeros = jnp.zeros_like(t)
        return jnp.concatenate([jnp.where(low, t, zeros), jnp.where(low, zeros, t)], axis=0)

    for ch, b in [(ch, b) for ch in range(chunks_per_step) for b in range(batch)]:
        rows = slice(ch * c, (ch + 1) * c)
        la = log_as[b][rows]
        hi = la.astype(BF16)
        rem = la - hi.astype(F32)
        mid = rem.astype(BF16)
        lo = (rem - mid.astype(F32)).astype(BF16)
        terms = _dot(tri, jnp.concatenate([hi, mid, lo], axis=1))
        bcum = terms[:, :dk2] + terms[:, dk2:2 * dk2] + terms[:, 2 * dk2:]
        b_last = bcum[c - 1:c, :]
        b_mid = bcum[c // 2 - 1:c // 2, :]
        qf = q_ref[b, rows, :] * (B_DK ** -0.5)
        kf = k_ref[b, rows, :]
        q_in = (qf * jnp.exp(bcum - b_mid)).astype(BF16)
        k_in = (kf * jnp.exp(b_mid - bcum)).astype(BF16)
        q_st = (qf * jnp.exp(bcum)).astype(BF16)
        k_st = (kf * jnp.exp(b_last - bcum)).astype(BF16)
        decay = jnp.exp(b_last)
        for pr in range(n_pairs):
            pc = slice(pr * LANES, (pr + 1) * LANES)
            st = st_ref[b, pr]
            vcols = slice(2 * pr * B_DV, (2 * pr + 2) * B_DV)
            v_lanes = v_ref[b, rows, vcols].astype(BF16)
            v_rows = jnp.concatenate([v_lanes[:, :B_DV], v_lanes[:, B_DV:]], axis=0)
            att = _dot_nt(stack_heads(q_in[:, pc]), stack_heads(k_in[:, pc]))
            att = jnp.where(causal2, att, 0.0)
            o = _dot(att.astype(BF16), v_rows) + _dot_nt(stack_heads(q_st[:, pc]), st.astype(BF16))
            o = _rms(o, norm_g)
            rg = r_ref[b, rows, vcols]
            gate_r = jax.nn.silu(jnp.concatenate([rg[:, :B_DV], rg[:, B_DV:]], axis=0))
            o = (o * gate_r).astype(o_ref.dtype)
            o_ref[b, rows, 2 * pr * B_DV:(2 * pr + 1) * B_DV] = o[:c]
            o_ref[b, rows, (2 * pr + 1) * B_DV:(2 * pr + 2) * B_DV] = o[c:]
            d_st = _dot_tn(v_lanes, k_st[:, pc])
            st_ref[b, pr] = st * decay[:, pc] + jnp.where(low_sq, d_st[:B_DV], d_st[B_DV:])


def _gla(qb, kb, vb, rb, gb, w_gate, b_gate, norm_g, batch):
    n = qb.shape[0]
    seq = n // batch
    t = min(256, seq)
    r3 = lambda a: a.reshape(batch, seq, a.shape[1])
    blk = lambda c: pl.BlockSpec((batch, t, c), lambda i: (0, i, 0))
    full = lambda a: pl.BlockSpec(a.shape, lambda i: (0,) * a.ndim)
    dk2 = B_HEADS * B_DK
    dv2 = B_HEADS * B_DV
    out = pl.pallas_call(
        functools.partial(_gla_kernel, chunks_per_step=t // B_CHUNK),
        grid=(seq // t,),
        in_specs=[blk(dk2), blk(dk2), blk(dv2), blk(dv2), blk(LANES),
                  full(w_gate), full(b_gate), full(norm_g)],
        out_specs=blk(dv2),
        out_shape=jax.ShapeDtypeStruct((batch, seq, dv2), BF16),
        scratch_shapes=[pltpu.VMEM((batch, B_HEADS // 2, B_DV, 2 * B_DK), F32)],
        compiler_params=_params(("arbitrary",)),
        name="gla",
    )(r3(qb), r3(kb), r3(vb), r3(rb), r3(gb), w_gate, b_gate, norm_g)
    return out.reshape(n, dv2)


def _proj_ffn_kernel(a_ref, b_ref, wa_ref, wb_ref, x_ref, gmix_ref, gpre_ref, wg_ref, wu_ref, wd_ref,
                     gffn_ref, x_out, *, tf):
    y = _dot(a_ref[...], wa_ref[...]) + _dot(b_ref[...], wb_ref[...])
    x1 = x_ref[...] + _rms(y, gmix_ref[...])
    h = _rms(x1, gpre_ref[...]).astype(BF16)
    f = None
    for c in range(wg_ref.shape[1] // tf):
        cols = slice(c * tf, (c + 1) * tf)
        g = _dot(h, wg_ref[:, cols])
        u = _dot(h, wu_ref[:, cols])
        part = _dot((jax.nn.silu(g) * u).astype(BF16), wd_ref[cols, :])
        f = part if f is None else f + part
    x_out[...] = x1 + _rms(f, gffn_ref[...])


def _proj_ffn(a, b, wa, wb, x2d, g_mix, g_pre, wg, wu, wd, g_ffn, tm, tf):
    n, d = x2d.shape
    ca, cb = a.shape[1], b.shape[1]
    dff = wg.shape[1]
    row = lambda c: pl.BlockSpec((tm, c), lambda i: (i, 0))
    resident = lambda r, c: pl.BlockSpec((r, c), lambda i: (0, 0), pipeline_mode=pl.Buffered(1))
    vec = lambda g: g.reshape(1, d)
    return pl.pallas_call(
        functools.partial(_proj_ffn_kernel, tf=tf),
        grid=(n // tm,),
        in_specs=[row(ca), row(cb), resident(ca, d), resident(cb, d), row(d), resident(1, d),
                  resident(1, d), resident(d, dff), resident(d, dff), resident(dff, d), resident(1, d)],
        out_specs=row(d),
        out_shape=jax.ShapeDtypeStruct((n, d), F32),
        compiler_params=_params(("parallel",)),
        name="proj_ffn",
    )(a, b, wa, wb, x2d, vec(g_mix), vec(g_pre), wg, wu, wd, vec(g_ffn))


def _mla_prep_kernel(cq_ref, ckv_ref, kr_ref, qn_ref, kvn_ref, wq_ref, wk_ref, wv_ref,
                     cosq_ref, sinq_ref, cosk_ref, sink_ref, q_out, k_out, v_out, *, scale):
    hq = _rms(cq_ref[...], qn_ref[...]).astype(BF16)
    hkv = _rms(ckv_ref[...], kvn_ref[...]).astype(BF16)
    lane = lax.broadcasted_iota(jnp.int32, (cq_ref.shape[0], LANES), 1)
    half = D_ROPE // 2

    def swap_halves(t, start):
        return jnp.where(lane < start + half, pltpu.roll(t, LANES - half, 1), pltpu.roll(t, half, 1))

    kr = kr_ref[...]
    kpe = kr * cosk_ref[...] + swap_halves(kr, 0) * sink_ref[...]
    kpe = pltpu.roll(kpe, D_NOPE, 1)
    cosq = cosq_ref[...]
    sinq = sinq_ref[...]
    for h in range(D_HEADS):
        cols = slice(h * LANES, (h + 1) * LANES)
        q = _dot(hq, wq_ref[:, cols])
        q = q * cosq + swap_halves(q, D_NOPE) * sinq
        q_out[:, cols] = (q * scale).astype(q_out.dtype)
        k = _dot(hkv, wk_ref[:, cols])
        k_out[:, cols] = (k + kpe).astype(k_out.dtype)
    for pr in range(D_HEADS // 2):
        v = _dot(hkv, wv_ref[:, pr * LANES:(pr + 1) * LANES])
        v_out[:, 2 * pr * LANES:(2 * pr + 1) * LANES] = v.astype(v_out.dtype)
        v_out[:, (2 * pr + 1) * LANES:(2 * pr + 2) * LANES] = jnp.ones(v.shape, v_out.dtype)


def _mla_prep(c_q, c_kv, kr, q_norm, kv_norm, wq, wk, wv, tables, seq, tm):
    n = c_q.shape[0]
    tiles_per_seq = seq // tm
    row = lambda c: pl.BlockSpec((tm, c), lambda i: (i, 0))
    const = lambda a: pl.BlockSpec(a.shape, lambda i: (0,) * a.ndim)
    tab = pl.BlockSpec((tm, LANES), lambda i: (i % tiles_per_seq, 0))
    hw = D_HEADS * LANES
    return pl.pallas_call(
        functools.partial(_mla_prep_kernel, scale=float((D_NOPE + D_ROPE) ** -0.5) * LOG2E),
        grid=(n // tm,),
        in_specs=[row(D_Q_RANK), row(D_KV_RANK), row(LANES), const(q_norm), const(kv_norm),
                  const(wq), const(wk), const(wv), tab, tab, tab, tab],
        out_specs=[row(hw), row(hw), row(hw)],
        out_shape=[jax.ShapeDtypeStruct((n, hw), BF16)] * 3,
        compiler_params=_params(("parallel",)),
        name="mla_prep",
    )(c_q, c_kv, kr, q_norm, kv_norm, wq, wk, wv, *tables)


def _rope_tables(seq):
    half = D_ROPE // 2
    freqs = np.float32(ROPE_BASE) ** (-np.arange(half, dtype=np.float32) / np.float32(half))
    ang = np.arange(seq, dtype=np.float32)[:, None] * freqs[None, :]
    cos, sin = jnp.asarray(np.cos(ang), F32), jnp.asarray(np.sin(ang), F32)
    ones = jnp.ones((seq, D_NOPE), F32)
    zeros = lambda c: jnp.zeros((seq, c), F32)
    pad = LANES - D_NOPE - D_ROPE
    cosq = jnp.concatenate([ones, cos, cos, zeros(pad)], axis=1)
    sinq = jnp.concatenate([zeros(D_NOPE), -sin, sin, zeros(pad)], axis=1)
    cosk = jnp.concatenate([cos, cos, zeros(LANES - D_ROPE)], axis=1)
    sink = jnp.concatenate([-sin, sin, zeros(LANES - D_ROPE)], axis=1)
    return cosq, sinq, cosk, sink


def _mla_attn_kernel(qe_ref, qo_ref, k_ref, v_ref, oe_ref, oo_ref, q_sc, s_ref, p_ref, m_ref, l_ref,
                     a_ref, acc_ref, *, tq, rb, n_full, group):
    tk = 2 * tq
    t_even = pl.program_id(2)
    q_sc[0] = qe_ref[0]
    q_sc[1] = qo_ref[0]
    m_ref[...] = jnp.full(m_ref.shape, NEG_INF, F32)
    l_ref[...] = jnp.zeros(l_ref.shape, F32)
    acc_ref[...] = jnp.zeros(acc_ref.shape, F32)

    diag_odd = (1, n_full - t_even, tk, "odd")
    diag_even = (0, t_even, tq, "even")

    def full_tile(u):
        sel = (u >= t_even).astype(jnp.int32)
        return (sel, u - sel * t_even, tk, None)

    def scores(tile, buf):
        sel, kt, width, mask = tile
        k0 = pl.multiple_of(kt * tk, tk)
        for e in range(2):
            qblk = q_sc[sel, :, e * LANES:(e + 1) * LANES]
            kblk = k_ref[0, pl.ds(k0, width), e * LANES:(e + 1) * LANES]
            s = _dot_nt(qblk, kblk)
            if mask is not None:
                row = lax.broadcasted_iota(jnp.int32, (tq, width), 0)
                col = lax.broadcasted_iota(jnp.int32, (tq, width), 1)
                s = jnp.where(row >= (col if mask == "even" else col - tq), s, NEG_INF)
            s_ref[buf, e, :, 0:width] = s

    def softmax(tile, buf):
        sel, _, width, _ = tile
        for e in range(2):
            for r in range(tq // rb):
                rows = slice(r * rb, (r + 1) * rb)
                m_old = m_ref[sel, e, rows, :]
                m_new = jnp.maximum(m_old, jnp.max(s_ref[buf, e, rows, 0:width], axis=-1, keepdims=True))
                a_ref[buf, e, rows, :] = jnp.exp2(m_old - m_new)
                m_ref[sel, e, rows, :] = m_new
            for r in range(tq // rb):
                rows = slice(r * rb, (r + 1) * rb)
                m_new = jnp.concatenate([m_ref[sel, e, rows, :]] * (width // LANES), axis=1)
                p_ref[buf, e, rows, 0:width] = jnp.exp2(s_ref[buf, e, rows, 0:width] - m_new).astype(BF16)

    def values(tile, buf):
        sel, kt, width, _ = tile
        k0 = pl.multiple_of(kt * tk, tk)
        vblk = v_ref[0, pl.ds(k0, width), :]
        for e in range(2):
            pv = _dot(p_ref[buf, e, :, 0:width], vblk)
            alpha = a_ref[buf, e]
            acc_ref[sel, e] = alpha * acc_ref[sel, e] + pv[:, :LANES]
            l_ref[sel, e] = alpha * l_ref[sel, e] + pv[:, LANES:]

    def step(t, prev, this, nxt):
        if nxt is not None:
            scores(nxt, (t + 1) % 2)
        if prev is not None:
            values(prev, (t - 1) % 2)
        softmax(this, t % 2)

    def static_tile(t):
        return (diag_odd, diag_even)[t] if t < 2 else full_tile(jnp.int32(t - 2))

    n_pos = n_full + 2
    loop_lo = min(3, n_pos)
    loop_iters = max(0, (n_pos - 1 - loop_lo) // group)
    loop_hi = loop_lo + group * loop_iters

    scores(diag_odd, 0)
    for t in range(loop_lo):
        step(t, static_tile(t - 1) if t >= 1 else None, static_tile(t),
             static_tile(t + 1) if t + 1 < n_pos else None)

    def body(jj, carry):
        for t_off in range(group):
            t = loop_lo + t_off
            u = group * jj + (loop_lo - 2) + t_off
            step(t, full_tile(u - 1), full_tile(u), full_tile(u + 1))
        return carry

    lax.fori_loop(0, loop_iters, body, 0)
    for t in range(loop_hi, n_pos):
        step(t, static_tile(t - 1), static_tile(t), static_tile(t + 1) if t + 1 < n_pos else None)
    values(static_tile(n_pos - 1), (n_pos - 1) % 2)

    lane = lax.broadcasted_iota(jnp.int32, (tq, LANES), 1)
    for sel, o_ref in ((0, oe_ref), (1, oo_ref)):
        outs = [acc_ref[sel, e] / l_ref[sel, e] for e in range(2)]
        o_ref[0] = jnp.where(lane < D_V, outs[0], outs[1]).astype(o_ref.dtype)


def _mla_attn(q, k, v, batch, tq):
    n = q.shape[0]
    seq = n // batch
    n_pairs = D_HEADS // 2
    half = seq // tq // 2
    q3 = q.reshape(batch, seq, q.shape[1])
    k3 = k.reshape(batch, seq, k.shape[1])
    v3 = v.reshape(batch, seq, v.shape[1])
    stat = pltpu.VMEM((2, 2, tq, LANES), F32)
    q_spec = lambda im: pl.BlockSpec((1, tq, 2 * LANES), im)
    kv_map = lambda b, p, i: (b, 0, p)
    k_spec = pl.BlockSpec((1, seq, 2 * LANES), kv_map)
    v_spec = pl.BlockSpec((1, seq, 2 * LANES), kv_map, pipeline_mode=pl.Buffered(1))
    o_shape = jax.ShapeDtypeStruct((batch, half * tq, n_pairs * LANES), BF16)
    o_even, o_odd = pl.pallas_call(
        functools.partial(_mla_attn_kernel, tq=tq, rb=16, n_full=half - 1, group=2),
        grid=(batch, n_pairs, half),
        in_specs=[q_spec(lambda b, p, i: (b, 2 * i, p)),
                  q_spec(lambda b, p, i: (b, 2 * half - 1 - 2 * i, p)),
                  k_spec, v_spec],
        out_specs=[pl.BlockSpec((1, tq, LANES), lambda b, p, i: (b, i, p)),
                   pl.BlockSpec((1, tq, LANES), lambda b, p, i: (b, half - 1 - i, p))],
        out_shape=[o_shape, o_shape],
        scratch_shapes=[pltpu.VMEM((2, tq, 2 * LANES), BF16),
                        pltpu.VMEM((2, 2, tq, 2 * tq), F32), pltpu.VMEM((2, 2, tq, 2 * tq), BF16),
                        stat, stat, stat, stat],
        compiler_params=_params(("parallel", "parallel", "arbitrary")),
        name="mla_attn",
    )(q3, q3, k3, v3)
    shape4 = (batch, half, tq, n_pairs * LANES)
    out = jnp.stack([o_even.reshape(shape4), o_odd.reshape(shape4)], axis=2)
    return out.reshape(n, n_pairs * LANES)


def _pad_cols(w, width):
    return jnp.pad(w, ((0, 0), (0, width - w.shape[1])))


def _layer0_w_in(w):
    return _pad_cols(w, 25 * LANES).astype(BF16)


def _layer1_w_in(w):
    hd = HEAD_DIM
    qc = w[:, :C_HEADS * hd]
    o = C_HEADS * hd
    kc = w[:, o:o + C_KV_HEADS * hd]
    o += C_KV_HEADS * hd
    vc = w[:, o:o + C_KV_HEADS * hd]
    o += C_KV_HEADS * hd
    rest = w[:, o:o + D_Q_RANK + D_KV_RANK]
    o += D_Q_RANK + D_KV_RANK
    kr = _pad_cols(w[:, o:o + D_ROPE], LANES)
    rep = lambda t: jnp.concatenate([t[:, g * hd:(g + 1) * hd] for g in range(C_KV_HEADS) for _ in range(2)], axis=1)
    return jnp.concatenate([qc, rep(kc), rep(vc), rest, kr], axis=1).astype(BF16)


def _mla_weights(w_uq, w_ukv):
    dq = D_NOPE + D_ROPE
    wq = w_uq.reshape(D_Q_RANK, D_HEADS, dq)
    wq = jnp.pad(wq, ((0, 0), (0, 0), (0, LANES - dq))).reshape(D_Q_RANK, D_HEADS * LANES)
    wkv = w_ukv.reshape(D_KV_RANK, D_HEADS, D_NOPE + D_V)
    wk = jnp.pad(wkv[:, :, :D_NOPE], ((0, 0), (0, 0), (0, LANES - D_NOPE)))
    wk = wk.reshape(D_KV_RANK, D_HEADS * LANES)
    wv = wkv[:, :, D_NOPE:].reshape(D_KV_RANK, D_HEADS * D_V)
    return wq.astype(BF16), wk.astype(BF16), wv.astype(BF16)


def _row_tile(n, want):
    return min(want, n)


def _splits(sizes, scales, slabs):
    starts = np.concatenate([[0], np.cumsum(sizes)[:-1]])
    return tuple((int(s), int(z), float(c), bool(b)) for s, z, c, b in zip(starts, sizes, scales, slabs))


def _mix_even(x2d, batch, g_pre, w_in, gla_w_gate_up, gla_b_gate, gla_norm):
    n = x2d.shape[0]
    ah = A_HEADS * HEAD_DIM
    dk2, dv2 = B_HEADS * B_DK, B_HEADS * B_DV
    sizes = (ah, ah, ah, dk2, dk2, dv2, dv2, LANES)
    scales = (HEAD_DIM ** -0.5 * LOG2E,) + (1.0,) * 7
    slabs = (True, True, True) + (False,) * 5
    qa, ka, va, qb, kb, vb, rb, gb = _norm_proj(x2d, g_pre, _layer0_w_in(w_in),
                                                _splits(sizes, scales, slabs), (F32,) * 8,
                                                _row_tile(n, 512))
    o_a = _band_attn(qa, ka, va, batch=batch,
                     patterns=tuple(sorted(((w // d, d) for w, d in A_PATTERNS), key=lambda p: -p[1])),
                     slopes=jnp.asarray(_alibi_slopes(A_HEADS) * np.float32(LOG2E)),
                     m0=jnp.full((A_HEADS,), NEG_INF, F32), l0=jnp.zeros((A_HEADS,), F32))
    w_gate = jnp.pad(gla_w_gate_up, ((0, LANES - B_GATE_RANK), (0, 0))).astype(BF16)
    o_b = _gla(qb, kb, vb, rb, gb, w_gate, gla_b_gate.reshape(1, dk2), gla_norm.reshape(1, B_DV), batch)
    return o_a, o_b


def _mix_odd(x2d, batch, g_pre, w_in, sinks, q_norm, w_uq, kv_norm, w_ukv):
    n = x2d.shape[0]
    seq = n // batch
    ch = C_HEADS * HEAD_DIM
    ckv = 2 * C_KV_HEADS * HEAD_DIM
    sizes = (ch, ckv, ckv, D_Q_RANK, D_KV_RANK, LANES)
    scales = (HEAD_DIM ** -0.5 * LOG2E,) + (1.0,) * 5
    slabs = (True, True, True, False, False, False)
    qc, kc, vc, c_q, c_kv, kr = _norm_proj(x2d, g_pre, _layer1_w_in(w_in),
                                           _splits(sizes, scales, slabs), (F32,) * 6,
                                           _row_tile(n, 512))
    o_c = _band_attn(qc, kc, vc, batch=batch, patterns=((C_WINDOW - 1, 1),),
                     slopes=jnp.asarray(_alibi_slopes(C_HEADS) * np.float32(LOG2E)),
                     m0=sinks.astype(F32) * LOG2E, l0=jnp.ones((C_HEADS,), F32))
    wq, wk, wv = _mla_weights(w_uq, w_ukv)
    q, k, v = _mla_prep(c_q, c_kv, kr, q_norm.reshape(1, D_Q_RANK), kv_norm.reshape(1, D_KV_RANK),
                        wq, wk, wv, _rope_tables(seq), seq, _row_tile(seq, 512))
    o_d = _mla_attn(q, k, v, batch, _row_tile(seq // 2, 512))
    return o_c, o_d


def kernel(x, norm_mix_pre, norm_mix_post, norm_ffn_pre, norm_ffn_post, ffn_w_gate, ffn_w_up, ffn_w_down, ab_w_in, ab_w_out, gla_w_gate_up, gla_b_gate, gla_norm, cd_w_in, cd_w_out, swa_sinks, mla_q_norm, mla_w_uq, mla_kv_norm, mla_w_ukv):
    batch, seq, d = x.shape
    n = batch * seq
    x2d = x.reshape(n, d)
    depth = norm_mix_pre.shape[0]
    for layer in range(depth):
        i = layer // 2
        if layer % 2 == 0:
            o1, o2 = _mix_even(x2d, batch, norm_mix_pre[layer], ab_w_in[i], gla_w_gate_up[i],
                               gla_b_gate[i], gla_norm[i])
            w_out = ab_w_out[i].astype(BF16)
        else:
            o1, o2 = _mix_odd(x2d, batch, norm_mix_pre[layer], cd_w_in[i], swa_sinks[i],
                              mla_q_norm[i], mla_w_uq[i], mla_kv_norm[i], mla_w_ukv[i])
            w_out = cd_w_out[i].astype(BF16)
        c1 = o1.shape[1]
        x2d = _proj_ffn(o1, o2, w_out[:c1], w_out[c1:], x2d, norm_mix_post[layer],
                        norm_ffn_pre[layer], ffn_w_gate[layer].astype(BF16),
                        ffn_w_up[layer].astype(BF16), ffn_w_down[layer].astype(BF16),
                        norm_ffn_post[layer], _row_tile(n, 512), 256)
    return x2d.reshape(batch, seq, d)
```

```python
import functools

import numpy as np
import jax
import jax.numpy as jnp
from jax import lax
from jax.experimental import pallas as pl
from jax.experimental.pallas import tpu as pltpu

F32 = jnp.float32
BF16 = jnp.bfloat16

LANES = 128
V7X_VMEM_BYTES = 64 * 1024 * 1024
VMEM_LIMIT = 56 * 1024 * 1024
LOG2E = float(np.log2(np.e))

D_MODEL = 1024
HEAD_DIM = 64
Q_BLOCK = 128
EPS = 1e-6
NEG_INF = -1e30
A_HEADS = 8
A_PATTERNS = ((128, 1), (512, 4), (2048, 16))
B_HEADS = 4
B_DK = 64
B_DV = 128
B_GATE_RANK = 16
B_GATE_TAU = 16.0
B_CHUNK = 64
C_HEADS = 8
C_KV_HEADS = 2
C_WINDOW = 128
D_HEADS = 8
D_Q_RANK = 384
D_KV_RANK = 256
D_NOPE = 64
D_ROPE = 32
D_V = 64
ROPE_BASE = 10000.0
D_FF = 2816


def _alibi_slopes(n):
    return np.array([2.0 ** (-8.0 * (i + 1) / n) for i in range(n)], dtype=np.float32)


def _params(semantics):
    return pltpu.CompilerParams(dimension_semantics=semantics, vmem_limit_bytes=VMEM_LIMIT)


def _rms(x, g):
    return x * lax.rsqrt(jnp.mean(x * x, axis=-1, keepdims=True) + EPS) * g


def _dot(a, b):
    return jnp.dot(a, b, preferred_element_type=F32)


def _dot_nt(a, b):
    return lax.dot_general(a, b, (((1,), (1,)), ((), ())), preferred_element_type=F32)


def _dot_tn(a, b):
    return lax.dot_general(a, b, (((0,), (0,)), ((), ())), preferred_element_type=F32)


def _norm_proj_kernel(x_ref, g_ref, w_ref, *out_refs, splits):
    h = _rms(x_ref[...], g_ref[...]).astype(BF16)
    for o_ref, (start, size, scale, slabs) in zip(out_refs, splits):
        y = _dot(h, w_ref[:, start:start + size])
        if scale != 1.0:
            y = y * scale
        if slabs:
            for j in range(size // LANES):
                o_ref[j] = y[:, j * LANES:(j + 1) * LANES].astype(o_ref.dtype)
        else:
            o_ref[...] = y.astype(o_ref.dtype)


def _norm_proj(x2d, g, w, splits, dtypes, tm):
    n, d = x2d.shape
    wn = w.shape[1]
    out_shape, out_specs = [], []
    for (_, size, _, slabs), dt in zip(splits, dtypes):
        if slabs:
            out_shape.append(jax.ShapeDtypeStruct((size // LANES, n, LANES), dt))
            out_specs.append(pl.BlockSpec((size // LANES, tm, LANES), lambda i: (0, i, 0)))
        else:
            out_shape.append(jax.ShapeDtypeStruct((n, size), dt))
            out_specs.append(pl.BlockSpec((tm, size), lambda i: (i, 0)))
    return pl.pallas_call(
        functools.partial(_norm_proj_kernel, splits=splits),
        grid=(n // tm,),
        in_specs=[pl.BlockSpec((tm, d), lambda i: (i, 0)),
                  pl.BlockSpec((1, d), lambda i: (0, 0)),
                  pl.BlockSpec((d, wn), lambda i: (0, 0))],
        out_specs=out_specs,
        out_shape=out_shape,
        compiler_params=_params(("parallel",)),
        name="norm_proj",
    )(x2d, g.reshape(1, d), w)


def _band_attn_kernel(slope_ref, m0_ref, l0_ref, q_ref, kc_ref, kp_ref, vc_ref, vp_ref, o_ref,
                      kcat, vcat, acc_ref, m_ref, l_ref, bias_ref, *, patterns, tt):
    hp = pl.program_id(1)
    first_tile = pl.program_id(2) == 0
    kcat[0:tt, :] = kp_ref[0]
    kcat[tt:, :] = kc_ref[0]
    vcat[0:tt, :] = vp_ref[0]
    vcat[tt:, :] = vc_ref[0]
    m_init = [jnp.full((Q_BLOCK, LANES), m0_ref[2 * hp + e], F32) for e in range(2)]
    l_init = [jnp.full((Q_BLOCK, LANES), l0_ref[2 * hp + e], F32) for e in range(2)]

    row = lax.broadcasted_iota(jnp.int32, (Q_BLOCK, 2 * Q_BLOCK), 0)
    col = lax.broadcasted_iota(jnp.int32, (Q_BLOCK, 2 * Q_BLOCK), 1)
    rel = row + Q_BLOCK - col
    lane = lax.broadcasted_iota(jnp.int32, (Q_BLOCK, LANES), 1)
    low = lane < HEAD_DIM
    ones = jnp.ones((2 * Q_BLOCK, LANES), BF16)

    for pi, (max_rel, d) in enumerate(patterns):
        first = pi == 0
        fused_out = pi == len(patterns) - 1 and d == 1
        band = (rel >= 0) & (rel <= max_rel)
        dist = (rel * d).astype(F32)
        for e in range(2):
            bias_ref[e] = slope_ref[2 * hp + e] * dist
        blocks_per_residue = tt // (Q_BLOCK * d)

        for idx in range(tt // Q_BLOCK):
            r = idx // blocks_per_residue
            t = idx % blocks_per_residue
            q0 = r + t * (Q_BLOCK * d)
            k0 = tt - Q_BLOCK * d + q0
            qrows = pl.ds(q0, Q_BLOCK, stride=d) if d > 1 else pl.ds(q0, Q_BLOCK)
            krows = pl.ds(k0, 2 * Q_BLOCK, stride=d) if d > 1 else pl.ds(k0, 2 * Q_BLOCK)
            valid = (band & ((col >= Q_BLOCK) | jnp.logical_not(first_tile))) if t == 0 else band
            qblk = q_ref[0, qrows, :].astype(BF16)
            kblk = kcat[krows, :].astype(BF16)
            vext = jnp.concatenate([vcat[krows, :].astype(BF16), ones], axis=1)
            alphas, pvs, ls = [], [], []
            for e in range(2):
                qm = jnp.where(low if e == 0 else jnp.logical_not(low), qblk, jnp.zeros_like(qblk))
                s = _dot_nt(qm, kblk)
                s = jnp.where(valid, s - bias_ref[e], NEG_INF)
                m_old = m_init[e] if first else m_ref[e, qrows, :]
                l_old = l_init[e] if first else l_ref[e, qrows, :]
                m_new = jnp.maximum(m_old, jnp.max(s, axis=-1, keepdims=True))
                alpha = jnp.exp2(m_old - m_new)
                p = jnp.exp2(s - jnp.concatenate([m_new, m_new], axis=1))
                pv = _dot(p.astype(BF16), vext)
                l_new = alpha * l_old + pv[:, LANES:]
                if not fused_out:
                    m_ref[e, qrows, :] = m_new
                    l_ref[e, qrows, :] = l_new
                alphas.append(alpha)
                pvs.append(pv[:, :LANES])
                ls.append(l_new)
            acc = jnp.where(low, pvs[0], pvs[1])
            if not first:
                acc = jnp.where(low, alphas[0], alphas[1]) * acc_ref[qrows, :] + acc
            if fused_out:
                o_ref[pl.ds(q0, Q_BLOCK), :] = (acc / jnp.where(low, ls[0], ls[1])).astype(o_ref.dtype)
            else:
                acc_ref[qrows, :] = acc

    if patterns[-1][1] != 1:
        lane_t = lax.broadcasted_iota(jnp.int32, (tt, LANES), 1)
        o_ref[...] = (acc_ref[...] / jnp.where(lane_t < HEAD_DIM, l_ref[0], l_ref[1])).astype(o_ref.dtype)


def _band_attn(q, k, v, *, batch, patterns, slopes, m0, l0):
    n_pairs, n, _ = q.shape
    n_kv = k.shape[0]
    seq = n // batch
    max_d = max(d for _, d in patterns)
    assert max_d == 1 or (q.dtype == F32 and k.dtype == F32 and v.dtype == F32)
    tt = min(seq, max(2048, Q_BLOCK * max_d))
    tiles = seq // tt
    cur = lambda b, p, i: (p, b * tiles + i, 0)
    cur_kv = lambda b, p, i: (p * n_kv // n_pairs, b * tiles + i, 0)
    prev_kv = lambda b, p, i: (p * n_kv // n_pairs, b * tiles + jnp.maximum(i - 1, 0), 0)
    smem = pl.BlockSpec(memory_space=pltpu.SMEM)
    blk = lambda im: pl.BlockSpec((1, tt, LANES), im)
    stat = pltpu.VMEM((2, tt, LANES), F32)
    return pl.pallas_call(
        functools.partial(_band_attn_kernel, patterns=patterns, tt=tt),
        grid=(batch, n_pairs, tiles),
        in_specs=[smem, smem, smem, blk(cur), blk(cur_kv), blk(prev_kv), blk(cur_kv), blk(prev_kv)],
        out_specs=pl.BlockSpec((tt, LANES), lambda b, p, i: (b * tiles + i, p)),
        out_shape=jax.ShapeDtypeStruct((n, n_pairs * LANES), BF16),
        scratch_shapes=[pltpu.VMEM((2 * tt, LANES), k.dtype), pltpu.VMEM((2 * tt, LANES), v.dtype),
                        pltpu.VMEM((tt, LANES), F32), stat, stat,
                        pltpu.VMEM((2, Q_BLOCK, 2 * Q_BLOCK), F32)],
        compiler_params=_params(("parallel", "parallel", "arbitrary")),
        name="band_attn_p%d" % len(patterns),
    )(slopes, m0, l0, q, k, k, v, v)


def _gla_kernel(q_ref, k_ref, v_ref, r_ref, g_ref, wg_ref, bg_ref, ng_ref, o_ref, st_ref, *,
                chunks_per_step):
    c = B_CHUNK

    @pl.when(pl.program_id(0) == 0)
    def _():
        st_ref[...] = jnp.zeros_like(st_ref)

    batch = q_ref.shape[0]
    log_as = []
    for b in range(batch):
        gate = _dot(g_ref[b].astype(BF16), wg_ref[...]) + bg_ref[...]
        log_as.append(jax.nn.log_sigmoid(gate) / B_GATE_TAU)
    ri = lax.broadcasted_iota(jnp.int32, (c, c), 0)
    ci = lax.broadcasted_iota(jnp.int32, (c, c), 1)
    tri = (ri >= ci).astype(BF16)
    lane = lax.broadcasted_iota(jnp.int32, (c, LANES), 1)
    low = lane < B_DK
    ri2 = lax.broadcasted_iota(jnp.int32, (2 * c, 2 * c), 0)
    ci2 = lax.broadcasted_iota(jnp.int32, (2 * c, 2 * c), 1)
    causal2 = ((ri2 < c) == (ci2 < c)) & ((ri2 % c) >= (ci2 % c))
    low_sq = ci2 < B_DK
    norm_g = ng_ref[...]
    n_pairs = B_HEADS // 2
    dk2 = B_HEADS * B_DK

    def stack_heads(t):
        zeros = jnp.zeros_like(t)
        return jnp.concatenate([jnp.where(low, t, zeros), jnp.where(low, zeros, t)], axis=0)

    for ch, b in [(ch, b) for ch in range(chunks_per_step) for b in range(batch)]:
        rows = slice(ch * c, (ch + 1) * c)
        la = log_as[b][rows]
        hi = la.astype(BF16)
        rem = la - hi.astype(F32)
        mid = rem.astype(BF16)
        lo = (rem - mid.astype(F32)).astype(BF16)
        terms = _dot(tri, jnp.concatenate([hi, mid, lo], axis=1))
        bcum = terms[:, :dk2] + terms[:, dk2:2 * dk2] + terms[:, 2 * dk2:]
        b_last = bcum[c - 1:c, :]
        b_mid = bcum[c // 2 - 1:c // 2, :]
        qf = q_ref[b, rows, :] * (B_DK ** -0.5)
        kf = k_ref[b, rows, :]
        q_in = (qf * jnp.exp(bcum - b_mid)).astype(BF16)
        k_in = (kf * jnp.exp(b_mid - bcum)).astype(BF16)
        q_st = (qf * jnp.exp(bcum)).astype(BF16)
        k_st = (kf * jnp.exp(b_last - bcum)).astype(BF16)
        decay = jnp.exp(b_last)
        for pr in range(n_pairs):
            pc = slice(pr * LANES, (pr + 1) * LANES)
            st = st_ref[b, pr]
            vcols = slice(2 * pr * B_DV, (2 * pr + 2) * B_DV)
            v_lanes = v_ref[b, rows, vcols].astype(BF16)
            v_rows = jnp.concatenate([v_lanes[:, :B_DV], v_lanes[:, B_DV:]], axis=0)
            att = _dot_nt(stack_heads(q_in[:, pc]), stack_heads(k_in[:, pc]))
            att = jnp.where(causal2, att, 0.0)
            o = _dot(att.astype(BF16), v_rows) + _dot_nt(stack_heads(q_st[:, pc]), st.astype(BF16))
            o = _rms(o, norm_g)
            rg = r_ref[b, rows, vcols]
            gate_r = jax.nn.silu(jnp.concatenate([rg[:, :B_DV], rg[:, B_DV:]], axis=0))
            o = (o * gate_r).astype(o_ref.dtype)
            o_ref[b, rows, 2 * pr * B_DV:(2 * pr + 1) * B_DV] = o[:c]
            o_ref[b, rows, (2 * pr + 1) * B_DV:(2 * pr + 2) * B_DV] = o[c:]
            d_st = _dot_tn(v_lanes, k_st[:, pc])
            st_ref[b, pr] = st * decay[:, pc] + jnp.where(low_sq, d_st[:B_DV], d_st[B_DV:])


def _gla(qb, kb, vb, rb, gb, w_gate, b_gate, norm_g, batch):
    n = qb.shape[0]
    seq = n // batch
    t = min(512, seq)
    r3 = lambda a: a.reshape(batch, seq, a.shape[1])
    blk = lambda c: pl.BlockSpec((batch, t, c), lambda i: (0, i, 0))
    full = lambda a: pl.BlockSpec(a.shape, lambda i: (0,) * a.ndim)
    dk2 = B_HEADS * B_DK
    dv2 = B_HEADS * B_DV
    out = pl.pallas_call(
        functools.partial(_gla_kernel, chunks_per_step=t // B_CHUNK),
        grid=(seq // t,),
        in_specs=[blk(dk2), blk(dk2), blk(dv2), blk(dv2), blk(LANES),
                  full(w_gate), full(b_gate), full(norm_g)],
        out_specs=blk(dv2),
        out_shape=jax.ShapeDtypeStruct((batch, seq, dv2), BF16),
        scratch_shapes=[pltpu.VMEM((batch, B_HEADS // 2, B_DV, 2 * B_DK), F32)],
        compiler_params=_params(("arbitrary",)),
        name="gla",
    )(r3(qb), r3(kb), r3(vb), r3(rb), r3(gb), w_gate, b_gate, norm_g)
    return out.reshape(n, dv2)


def _proj_ffn_kernel(a_ref, b_ref, wa_ref, wb_ref, x_ref, gmix_ref, gpre_ref, wg_ref, wu_ref, wd_ref,
                     gffn_ref, x_out, *, tf):
    y = _dot(a_ref[...], wa_ref[...]) + _dot(b_ref[...], wb_ref[...])
    x1 = x_ref[...] + _rms(y, gmix_ref[...])
    h = _rms(x1, gpre_ref[...]).astype(BF16)
    f = None
    for c in range(wg_ref.shape[1] // tf):
        cols = slice(c * tf, (c + 1) * tf)
        g = _dot(h, wg_ref[:, cols])
        u = _dot(h, wu_ref[:, cols])
        part = _dot((jax.nn.silu(g) * u).astype(BF16), wd_ref[cols, :])
        f = part if f is None else f + part
    x_out[...] = x1 + _rms(f, gffn_ref[...])


def _proj_ffn(a, b, wa, wb, x2d, g_mix, g_pre, wg, wu, wd, g_ffn, tm, tf):
    n, d = x2d.shape
    ca, cb = a.shape[1], b.shape[1]
    dff = wg.shape[1]
    row = lambda c: pl.BlockSpec((tm, c), lambda i: (i, 0))
    resident = lambda r, c: pl.BlockSpec((r, c), lambda i: (0, 0), pipeline_mode=pl.Buffered(1))
    vec = lambda g: g.reshape(1, d)
    return pl.pallas_call(
        functools.partial(_proj_ffn_kernel, tf=tf),
        grid=(n // tm,),
        in_specs=[row(ca), row(cb), resident(ca, d), resident(cb, d), row(d), resident(1, d),
                  resident(1, d), resident(d, dff), resident(d, dff), resident(dff, d), resident(1, d)],
        out_specs=row(d),
        out_shape=jax.ShapeDtypeStruct((n, d), F32),
        compiler_params=_params(("parallel",)),
        name="proj_ffn",
    )(a, b, wa, wb, x2d, vec(g_mix), vec(g_pre), wg, wu, wd, vec(g_ffn))


def _norm_proj_mla_kernel(x_ref, g_ref, w_ref, qn_ref, kvn_ref, wq_ref, wk_ref, wv_ref,
                          cosq_ref, sinq_ref, cosk_ref, sink_ref,
                          qc_out, kc_out, vc_out, q_out, k_out, v_out, *, splits, scale):
    h = _rms(x_ref[...], g_ref[...]).astype(BF16)
    parts = []
    for start, size, part_scale, _ in splits:
        y = _dot(h, w_ref[:, start:start + size])
        parts.append(y * part_scale if part_scale != 1.0 else y)
    for o_ref, y in zip((qc_out, kc_out, vc_out), parts[:3]):
        for j in range(y.shape[1] // LANES):
            o_ref[j] = y[:, j * LANES:(j + 1) * LANES].astype(o_ref.dtype)
    c_q, c_kv, kr = parts[3:]
    hq = _rms(c_q, qn_ref[...]).astype(BF16)
    hkv = _rms(c_kv, kvn_ref[...]).astype(BF16)
    lane = lax.broadcasted_iota(jnp.int32, (x_ref.shape[0], LANES), 1)
    half = D_ROPE // 2

    def swap_halves(t, start):
        return jnp.where(lane < start + half, pltpu.roll(t, LANES - half, 1), pltpu.roll(t, half, 1))

    kpe = kr * cosk_ref[...] + swap_halves(kr, 0) * sink_ref[...]
    kpe = pltpu.roll(kpe, D_NOPE, 1)
    cosq = cosq_ref[...]
    sinq = sinq_ref[...]
    for h in range(D_HEADS):
        cols = slice(h * LANES, (h + 1) * LANES)
        q = _dot(hq, wq_ref[:, cols])
        q = q * cosq + swap_halves(q, D_NOPE) * sinq
        q_out[:, cols] = (q * scale).astype(q_out.dtype)
        k = _dot(hkv, wk_ref[:, cols])
        k_out[:, cols] = (k + kpe).astype(k_out.dtype)
    for pr in range(D_HEADS // 2):
        v = _dot(hkv, wv_ref[:, pr * LANES:(pr + 1) * LANES])
        v_out[:, 2 * pr * LANES:(2 * pr + 1) * LANES] = v.astype(v_out.dtype)
        v_out[:, (2 * pr + 1) * LANES:(2 * pr + 2) * LANES] = jnp.ones(v.shape, v_out.dtype)


def _norm_proj_mla(x2d, g, w, splits, slab_dtype, q_norm, kv_norm, wq, wk, wv, tables, seq, tm):
    n, d = x2d.shape
    tiles_per_seq = seq // tm
    row = lambda c: pl.BlockSpec((tm, c), lambda i: (i, 0))
    const = lambda a: pl.BlockSpec(a.shape, lambda i: (0,) * a.ndim)
    tab = pl.BlockSpec((tm, LANES), lambda i: (i % tiles_per_seq, 0))
    slab = lambda size: pl.BlockSpec((size // LANES, tm, LANES), lambda i: (0, i, 0))
    slab_shape = lambda size: jax.ShapeDtypeStruct((size // LANES, n, LANES), slab_dtype)
    hw = D_HEADS * LANES
    slab_sizes = [size for _, size, _, _ in splits[:3]]
    g2 = g.reshape(1, d)
    return pl.pallas_call(
        functools.partial(_norm_proj_mla_kernel, splits=splits,
                          scale=float((D_NOPE + D_ROPE) ** -0.5) * LOG2E),
        grid=(n // tm,),
        in_specs=[row(d), const(g2), const(w), const(q_norm), const(kv_norm),
                  const(wq), const(wk), const(wv), tab, tab, tab, tab],
        out_specs=[slab(s) for s in slab_sizes] + [row(hw), row(hw), row(hw)],
        out_shape=[slab_shape(s) for s in slab_sizes] + [jax.ShapeDtypeStruct((n, hw), BF16)] * 3,
        compiler_params=_params(("parallel",)),
        name="norm_proj_mla",
    )(x2d, g2, w, q_norm, kv_norm, wq, wk, wv, *tables)


def _rope_tables(seq):
    half = D_ROPE // 2
    freqs = np.float32(ROPE_BASE) ** (-np.arange(half, dtype=np.float32) / np.float32(half))
    ang = np.arange(seq, dtype=np.float32)[:, None] * freqs[None, :]
    cos, sin = jnp.asarray(np.cos(ang), F32), jnp.asarray(np.sin(ang), F32)
    ones = jnp.ones((seq, D_NOPE), F32)
    zeros = lambda c: jnp.zeros((seq, c), F32)
    pad = LANES - D_NOPE - D_ROPE
    cosq = jnp.concatenate([ones, cos, cos, zeros(pad)], axis=1)
    sinq = jnp.concatenate([zeros(D_NOPE), -sin, sin, zeros(pad)], axis=1)
    cosk = jnp.concatenate([cos, cos, zeros(LANES - D_ROPE)], axis=1)
    sink = jnp.concatenate([-sin, sin, zeros(LANES - D_ROPE)], axis=1)
    return cosq, sinq, cosk, sink


def _mla_attn_kernel(qe_ref, qo_ref, k_ref, v_ref, oe_ref, oo_ref, q_sc, s_ref, p_ref, m_ref, l_ref,
                     a_ref, acc_ref, *, tq, rb, n_full, group):
    tk = 2 * tq
    t_even = pl.program_id(2)
    q_sc[0] = qe_ref[0]
    q_sc[1] = qo_ref[0]
    m_ref[...] = jnp.full(m_ref.shape, NEG_INF, F32)
    l_ref[...] = jnp.zeros(l_ref.shape, F32)
    acc_ref[...] = jnp.zeros(acc_ref.shape, F32)

    diag_odd = (1, n_full - t_even, tk, "odd")
    diag_even = (0, t_even, tq, "even")

    def full_tile(u):
        sel = (u >= t_even).astype(jnp.int32)
        return (sel, u - sel * t_even, tk, None)

    def scores(tile, buf):
        sel, kt, width, mask = tile
        k0 = pl.multiple_of(kt * tk, tk)
        for e in range(2):
            qblk = q_sc[sel, :, e * LANES:(e + 1) * LANES]
            kblk = k_ref[0, pl.ds(k0, width), e * LANES:(e + 1) * LANES]
            s = _dot_nt(qblk, kblk)
            if mask is not None:
                row = lax.broadcasted_iota(jnp.int32, (tq, width), 0)
                col = lax.broadcasted_iota(jnp.int32, (tq, width), 1)
                s = jnp.where(row >= (col if mask == "even" else col - tq), s, NEG_INF)
            s_ref[buf, e, :, 0:width] = s

    def softmax(tile, buf):
        sel, _, width, _ = tile
        for e in range(2):
            for r in range(tq // rb):
                rows = slice(r * rb, (r + 1) * rb)
                m_old = m_ref[sel, e, rows, :]
                m_new = jnp.maximum(m_old, jnp.max(s_ref[buf, e, rows, 0:width], axis=-1, keepdims=True))
                a_ref[buf, e, rows, :] = jnp.exp2(m_old - m_new)
                m_ref[sel, e, rows, :] = m_new
            for r in range(tq // rb):
                rows = slice(r * rb, (r + 1) * rb)
                m_new = jnp.concatenate([m_ref[sel, e, rows, :]] * (width // LANES), axis=1)
                p_ref[buf, e, rows, 0:width] = jnp.exp2(s_ref[buf, e, rows, 0:width] - m_new).astype(BF16)

    def values(tile, buf):
        sel, kt, width, _ = tile
        k0 = pl.multiple_of(kt * tk, tk)
        vblk = v_ref[0, pl.ds(k0, width), :]
        for e in range(2):
            pv = _dot(p_ref[buf, e, :, 0:width], vblk)
            alpha = a_ref[buf, e]
            acc_ref[sel, e] = alpha * acc_ref[sel, e] + pv[:, :LANES]
            l_ref[sel, e] = alpha * l_ref[sel, e] + pv[:, LANES:]

    def step(t, prev, this, nxt):
        if nxt is not None:
            scores(nxt, (t + 1) % 2)
        if prev is not None:
            values(prev, (t - 1) % 2)
        softmax(this, t % 2)

    def static_tile(t):
        return (diag_odd, diag_even)[t] if t < 2 else full_tile(jnp.int32(t - 2))

    n_pos = n_full + 2
    loop_lo = min(3, n_pos)
    loop_iters = max(0, (n_pos - 1 - loop_lo) // group)
    loop_hi = loop_lo + group * loop_iters

    scores(diag_odd, 0)
    for t in range(loop_lo):
        step(t, static_tile(t - 1) if t >= 1 else None, static_tile(t),
             static_tile(t + 1) if t + 1 < n_pos else None)

    def body(jj, carry):
        for t_off in range(group):
            t = loop_lo + t_off
            u = group * jj + (loop_lo - 2) + t_off
            step(t, full_tile(u - 1), full_tile(u), full_tile(u + 1))
        return carry

    lax.fori_loop(0, loop_iters, body, 0)
    for t in range(loop_hi, n_pos):
        step(t, static_tile(t - 1), static_tile(t), static_tile(t + 1) if t + 1 < n_pos else None)
    values(static_tile(n_pos - 1), (n_pos - 1) % 2)

    lane = lax.broadcasted_iota(jnp.int32, (tq, LANES), 1)
    for sel, o_ref in ((0, oe_ref), (1, oo_ref)):
        outs = [acc_ref[sel, e] / l_ref[sel, e] for e in range(2)]
        o_ref[0] = jnp.where(lane < D_V, outs[0], outs[1]).astype(o_ref.dtype)


def _mla_attn(q, k, v, batch, tq):
    n = q.shape[0]
    seq = n // batch
    n_pairs = D_HEADS // 2
    half = seq // tq // 2
    q3 = q.reshape(batch, seq, q.shape[1])
    k3 = k.reshape(batch, seq, k.shape[1])
    v3 = v.reshape(batch, seq, v.shape[1])
    stat = pltpu.VMEM((2, 2, tq, LANES), F32)
    q_spec = lambda im: pl.BlockSpec((1, tq, 2 * LANES), im)
    kv_map = lambda b, p, i: (b, 0, p)
    k_spec = pl.BlockSpec((1, seq, 2 * LANES), kv_map)
    v_spec = pl.BlockSpec((1, seq, 2 * LANES), kv_map, pipeline_mode=pl.Buffered(1))
    o_shape = jax.ShapeDtypeStruct((batch, half * tq, n_pairs * LANES), BF16)
    o_even, o_odd = pl.pallas_call(
        functools.partial(_mla_attn_kernel, tq=tq, rb=16, n_full=half - 1, group=2),
        grid=(batch, n_pairs, half),
        in_specs=[q_spec(lambda b, p, i: (b, 2 * i, p)),
                  q_spec(lambda b, p, i: (b, 2 * half - 1 - 2 * i, p)),
                  k_spec, v_spec],
        out_specs=[pl.BlockSpec((1, tq, LANES), lambda b, p, i: (b, i, p)),
                   pl.BlockSpec((1, tq, LANES), lambda b, p, i: (b, half - 1 - i, p))],
        out_shape=[o_shape, o_shape],
        scratch_shapes=[pltpu.VMEM((2, tq, 2 * LANES), BF16),
                        pltpu.VMEM((2, 2, tq, 2 * tq), F32), pltpu.VMEM((2, 2, tq, 2 * tq), BF16),
                        stat, stat, stat, stat],
        compiler_params=_params(("parallel", "parallel", "arbitrary")),
        name="mla_attn",
    )(q3, q3, k3, v3)
    shape4 = (batch, half, tq, n_pairs * LANES)
    out = jnp.stack([o_even.reshape(shape4), o_odd.reshape(shape4)], axis=2)
    return out.reshape(n, n_pairs * LANES)


def _pad_cols(w, width):
    return jnp.pad(w, ((0, 0), (0, width - w.shape[1])))


def _layer0_w_in(w):
    return _pad_cols(w, 25 * LANES).astype(BF16)


def _layer1_w_in(w):
    hd = HEAD_DIM
    qc = w[:, :C_HEADS * hd]
    o = C_HEADS * hd
    kc = w[:, o:o + C_KV_HEADS * hd]
    o += C_KV_HEADS * hd
    vc = w[:, o:o + C_KV_HEADS * hd]
    o += C_KV_HEADS * hd
    rest = w[:, o:o + D_Q_RANK + D_KV_RANK]
    o += D_Q_RANK + D_KV_RANK
    kr = _pad_cols(w[:, o:o + D_ROPE], LANES)
    rep = lambda t: jnp.concatenate([t[:, g * hd:(g + 1) * hd] for g in range(C_KV_HEADS) for _ in range(2)], axis=1)
    return jnp.concatenate([qc, rep(kc), rep(vc), rest, kr], axis=1).astype(BF16)


def _mla_weights(w_uq, w_ukv):
    dq = D_NOPE + D_ROPE
    wq = w_uq.reshape(D_Q_RANK, D_HEADS, dq)
    wq = jnp.pad(wq, ((0, 0), (0, 0), (0, LANES - dq))).reshape(D_Q_RANK, D_HEADS * LANES)
    wkv = w_ukv.reshape(D_KV_RANK, D_HEADS, D_NOPE + D_V)
    wk = jnp.pad(wkv[:, :, :D_NOPE], ((0, 0), (0, 0), (0, LANES - D_NOPE)))
    wk = wk.reshape(D_KV_RANK, D_HEADS * LANES)
    wv = wkv[:, :, D_NOPE:].reshape(D_KV_RANK, D_HEADS * D_V)
    return wq.astype(BF16), wk.astype(BF16), wv.astype(BF16)


def _row_tile(n, want):
    return min(want, n)


def _splits(sizes, scales, slabs):
    starts = np.concatenate([[0], np.cumsum(sizes)[:-1]])
    return tuple((int(s), int(z), float(c), bool(b)) for s, z, c, b in zip(starts, sizes, scales, slabs))


def _mix_even(x2d, batch, g_pre, w_in, gla_w_gate_up, gla_b_gate, gla_norm):
    n = x2d.shape[0]
    ah = A_HEADS * HEAD_DIM
    dk2, dv2 = B_HEADS * B_DK, B_HEADS * B_DV
    sizes = (ah, ah, ah, dk2, dk2, dv2, dv2, LANES)
    scales = (HEAD_DIM ** -0.5 * LOG2E,) + (1.0,) * 7
    slabs = (True, True, True) + (False,) * 5
    qa, ka, va, qb, kb, vb, rb, gb = _norm_proj(x2d, g_pre, _layer0_w_in(w_in),
                                                _splits(sizes, scales, slabs),
                                                (F32, F32, F32, F32, F32, BF16, F32, F32),
                                                _row_tile(n, 512))
    o_a = _band_attn(qa, ka, va, batch=batch,
                     patterns=tuple(sorted(((w // d, d) for w, d in A_PATTERNS), key=lambda p: -p[1])),
                     slopes=jnp.asarray(_alibi_slopes(A_HEADS) * np.float32(LOG2E)),
                     m0=jnp.full((A_HEADS,), NEG_INF, F32), l0=jnp.zeros((A_HEADS,), F32))
    w_gate = jnp.pad(gla_w_gate_up, ((0, LANES - B_GATE_RANK), (0, 0))).astype(BF16)
    o_b = _gla(qb, kb, vb, rb, gb, w_gate, gla_b_gate.reshape(1, dk2), gla_norm.reshape(1, B_DV), batch)
    return o_a, o_b


def _mix_odd(x2d, batch, g_pre, w_in, sinks, q_norm, w_uq, kv_norm, w_ukv):
    n = x2d.shape[0]
    seq = n // batch
    ch = C_HEADS * HEAD_DIM
    ckv = 2 * C_KV_HEADS * HEAD_DIM
    sizes = (ch, ckv, ckv, D_Q_RANK, D_KV_RANK, LANES)
    scales = (HEAD_DIM ** -0.5 * LOG2E,) + (1.0,) * 5
    slabs = (True, True, True, False, False, False)
    wq, wk, wv = _mla_weights(w_uq, w_ukv)
    qc, kc, vc, q, k, v = _norm_proj_mla(x2d, g_pre, _layer1_w_in(w_in), _splits(sizes, scales, slabs),
                                         BF16, q_norm.reshape(1, D_Q_RANK), kv_norm.reshape(1, D_KV_RANK),
                                         wq, wk, wv, _rope_tables(seq), seq, _row_tile(seq, 512))
    o_c = _band_attn(qc, kc, vc, batch=batch, patterns=((C_WINDOW - 1, 1),),
                     slopes=jnp.asarray(_alibi_slopes(C_HEADS) * np.float32(LOG2E)),
                     m0=sinks.astype(F32) * LOG2E, l0=jnp.ones((C_HEADS,), F32))
    o_d = _mla_attn(q, k, v, batch, _row_tile(seq // 2, 512))
    return o_c, o_d


def kernel(x, norm_mix_pre, norm_mix_post, norm_ffn_pre, norm_ffn_post, ffn_w_gate, ffn_w_up, ffn_w_down, ab_w_in, ab_w_out, gla_w_gate_up, gla_b_gate, gla_norm, cd_w_in, cd_w_out, swa_sinks, mla_q_norm, mla_w_uq, mla_kv_norm, mla_w_ukv):
    batch, seq, d = x.shape
    n = batch * seq
    x2d = x.reshape(n, d)
    depth = norm_mix_pre.shape[0]
    for layer in range(depth):
        i = layer // 2
        if layer % 2 == 0:
            o1, o2 = _mix_even(x2d, batch, norm_mix_pre[layer], ab_w_in[i], gla_w_gate_up[i],
                               gla_b_gate[i], gla_norm[i])
            w_out = ab_w_out[i].astype(BF16)
        else:
            o1, o2 = _mix_odd(x2d, batch, norm_mix_pre[layer], cd_w_in[i], swa_sinks[i],
                              mla_q_norm[i], mla_w_uq[i], mla_kv_norm[i], mla_w_ukv[i])
            w_out = cd_w_out[i].astype(BF16)
        c1 = o1.shape[1]
        x2d = _proj_ffn(o1, o2, w_out[:c1], w_out[c1:], x2d, norm_mix_post[layer],
                        norm_ffn_pre[layer], ffn_w_gate[layer].astype(BF16),
                        ffn_w_up[layer].astype(BF16), ffn_w_down[layer].astype(BF16),
                        norm_ffn_post[layer], _row_tile(n, 512), 256)
    return x2d.reshape(batch, seq, d)
```

```python
import functools

import numpy as np
import jax
import jax.numpy as jnp
from jax import lax
from jax.experimental import pallas as pl
from jax.experimental.pallas import tpu as pltpu

F32 = jnp.float32
BF16 = jnp.bfloat16

LANES = 128
V7X_VMEM_BYTES = 64 * 1024 * 1024
VMEM_LIMIT = 56 * 1024 * 1024
LOG2E = float(np.log2(np.e))

D_MODEL = 1024
HEAD_DIM = 64
Q_BLOCK = 128
EPS = 1e-6
NEG_INF = -1e30
A_HEADS = 8
A_PATTERNS = ((128, 1), (512, 4), (2048, 16))
B_HEADS = 4
B_DK = 64
B_DV = 128
B_GATE_RANK = 16
B_GATE_TAU = 16.0
B_CHUNK = 64
C_HEADS = 8
C_KV_HEADS = 2
C_WINDOW = 128
D_HEADS = 8
D_Q_RANK = 384
D_KV_RANK = 256
D_NOPE = 64
D_ROPE = 32
D_V = 64
ROPE_BASE = 10000.0
D_FF = 2816


def _alibi_slopes(n):
    return np.array([2.0 ** (-8.0 * (i + 1) / n) for i in range(n)], dtype=np.float32)


def _params(semantics):
    return pltpu.CompilerParams(dimension_semantics=semantics, vmem_limit_bytes=VMEM_LIMIT)


def _rms(x, g):
    return x * lax.rsqrt(jnp.mean(x * x, axis=-1, keepdims=True) + EPS) * g


def _dot(a, b):
    return jnp.dot(a, b, preferred_element_type=F32)


def _dot_nt(a, b):
    return lax.dot_general(a, b, (((1,), (1,)), ((), ())), preferred_element_type=F32)


def _dot_tn(a, b):
    return lax.dot_general(a, b, (((0,), (0,)), ((), ())), preferred_element_type=F32)


def _norm_proj_kernel(x_ref, g_ref, w_ref, *out_refs, splits):
    h = _rms(x_ref[...], g_ref[...]).astype(BF16)
    for o_ref, (start, size, scale, slabs) in zip(out_refs, splits):
        y = _dot(h, w_ref[:, start:start + size])
        if scale != 1.0:
            y = y * scale
        if slabs:
            for j in range(size // LANES):
                o_ref[j] = y[:, j * LANES:(j + 1) * LANES].astype(o_ref.dtype)
        else:
            o_ref[...] = y.astype(o_ref.dtype)


def _norm_proj(x2d, g, w, splits, dtypes, tm):
    n, d = x2d.shape
    wn = w.shape[1]
    out_shape, out_specs = [], []
    for (_, size, _, slabs), dt in zip(splits, dtypes):
        if slabs:
            out_shape.append(jax.ShapeDtypeStruct((size // LANES, n, LANES), dt))
            out_specs.append(pl.BlockSpec((size // LANES, tm, LANES), lambda i: (0, i, 0)))
        else:
            out_shape.append(jax.ShapeDtypeStruct((n, size), dt))
            out_specs.append(pl.BlockSpec((tm, size), lambda i: (i, 0)))
    return pl.pallas_call(
        functools.partial(_norm_proj_kernel, splits=splits),
        grid=(n // tm,),
        in_specs=[pl.BlockSpec((tm, d), lambda i: (i, 0)),
                  pl.BlockSpec((1, d), lambda i: (0, 0)),
                  pl.BlockSpec((d, wn), lambda i: (0, 0))],
        out_specs=out_specs,
        out_shape=out_shape,
        compiler_params=_params(("parallel",)),
        name="norm_proj",
    )(x2d, g.reshape(1, d), w)


def _band_attn_kernel(slope_ref, m0_ref, l0_ref, q_ref, kc_ref, kp_ref, vc_ref, vp_ref, o_ref,
                      kcat, vcat, acc_ref, m_ref, l_ref, bias_ref, *, patterns, tt):
    hp = pl.program_id(1)
    first_tile = pl.program_id(2) == 0
    kcat[0:tt, :] = kp_ref[0]
    kcat[tt:, :] = kc_ref[0]
    vcat[0:tt, :] = vp_ref[0]
    vcat[tt:, :] = vc_ref[0]
    m_init = [jnp.full((Q_BLOCK, LANES), m0_ref[2 * hp + e], F32) for e in range(2)]
    l_init = [jnp.full((Q_BLOCK, LANES), l0_ref[2 * hp + e], F32) for e in range(2)]

    row = lax.broadcasted_iota(jnp.int32, (Q_BLOCK, 2 * Q_BLOCK), 0)
    col = lax.broadcasted_iota(jnp.int32, (Q_BLOCK, 2 * Q_BLOCK), 1)
    rel = row + Q_BLOCK - col
    lane = lax.broadcasted_iota(jnp.int32, (Q_BLOCK, LANES), 1)
    low = lane < HEAD_DIM
    ones = jnp.ones((2 * Q_BLOCK, LANES), BF16)

    for pi, (max_rel, d) in enumerate(patterns):
        first = pi == 0
        fused_out = pi == len(patterns) - 1 and d == 1
        band = (rel >= 0) & (rel <= max_rel)
        dist = (rel * d).astype(F32)
        for e in range(2):
            bias_ref[e] = slope_ref[2 * hp + e] * dist
        blocks_per_residue = tt // (Q_BLOCK * d)

        for idx in range(tt // Q_BLOCK):
            r = idx // blocks_per_residue
            t = idx % blocks_per_residue
            q0 = r + t * (Q_BLOCK * d)
            k0 = tt - Q_BLOCK * d + q0
            qrows = pl.ds(q0, Q_BLOCK, stride=d) if d > 1 else pl.ds(q0, Q_BLOCK)
            krows = pl.ds(k0, 2 * Q_BLOCK, stride=d) if d > 1 else pl.ds(k0, 2 * Q_BLOCK)
            valid = (band & ((col >= Q_BLOCK) | jnp.logical_not(first_tile))) if t == 0 else band
            qblk = q_ref[0, qrows, :].astype(BF16)
            kblk = kcat[krows, :].astype(BF16)
            vext = jnp.concatenate([vcat[krows, :].astype(BF16), ones], axis=1)
            alphas, pvs, ls = [], [], []
            for e in range(2):
                qm = jnp.where(low if e == 0 else jnp.logical_not(low), qblk, jnp.zeros_like(qblk))
                s = _dot_nt(qm, kblk)
                s = jnp.where(valid, s - bias_ref[e], NEG_INF)
                m_old = m_init[e] if first else m_ref[e, qrows, :]
                l_old = l_init[e] if first else l_ref[e, qrows, :]
                m_new = jnp.maximum(m_old, jnp.max(s, axis=-1, keepdims=True))
                alpha = jnp.exp2(m_old - m_new)
                p = jnp.exp2(s - jnp.concatenate([m_new, m_new], axis=1))
                pv = _dot(p.astype(BF16), vext)
                l_new = alpha * l_old + pv[:, LANES:]
                if not fused_out:
                    m_ref[e, qrows, :] = m_new
                    l_ref[e, qrows, :] = l_new
                alphas.append(alpha)
                pvs.append(pv[:, :LANES])
                ls.append(l_new)
            acc = jnp.where(low, pvs[0], pvs[1])
            if not first:
                acc = jnp.where(low, alphas[0], alphas[1]) * acc_ref[qrows, :] + acc
            if fused_out:
                o_ref[pl.ds(q0, Q_BLOCK), :] = (acc / jnp.where(low, ls[0], ls[1])).astype(o_ref.dtype)
            else:
                acc_ref[qrows, :] = acc

    if patterns[-1][1] != 1:
        lane_t = lax.broadcasted_iota(jnp.int32, (tt, LANES), 1)
        o_ref[...] = (acc_ref[...] / jnp.where(lane_t < HEAD_DIM, l_ref[0], l_ref[1])).astype(o_ref.dtype)


def _band_attn(q, k, v, *, batch, patterns, slopes, m0, l0):
    n_pairs, n, _ = q.shape
    n_kv = k.shape[0]
    seq = n // batch
    max_d = max(d for _, d in patterns)
    assert max_d == 1 or (q.dtype == F32 and k.dtype == F32 and v.dtype == F32)
    tt = min(seq, max(2048, Q_BLOCK * max_d))
    tiles = seq // tt
    cur = lambda b, p, i: (p, b * tiles + i, 0)
    cur_kv = lambda b, p, i: (p * n_kv // n_pairs, b * tiles + i, 0)
    prev_kv = lambda b, p, i: (p * n_kv // n_pairs, b * tiles + jnp.maximum(i - 1, 0), 0)
    smem = pl.BlockSpec(memory_space=pltpu.SMEM)
    blk = lambda im: pl.BlockSpec((1, tt, LANES), im)
    stat = pltpu.VMEM((2, tt, LANES), F32)
    return pl.pallas_call(
        functools.partial(_band_attn_kernel, patterns=patterns, tt=tt),
        grid=(batch, n_pairs, tiles),
        in_specs=[smem, smem, smem, blk(cur), blk(cur_kv), blk(prev_kv), blk(cur_kv), blk(prev_kv)],
        out_specs=pl.BlockSpec((tt, LANES), lambda b, p, i: (b * tiles + i, p)),
        out_shape=jax.ShapeDtypeStruct((n, n_pairs * LANES), BF16),
        scratch_shapes=[pltpu.VMEM((2 * tt, LANES), k.dtype), pltpu.VMEM((2 * tt, LANES), v.dtype),
                        pltpu.VMEM((tt, LANES), F32), stat, stat,
                        pltpu.VMEM((2, Q_BLOCK, 2 * Q_BLOCK), F32)],
        compiler_params=_params(("parallel", "parallel", "arbitrary")),
        name="band_attn_p%d" % len(patterns),
    )(slopes, m0, l0, q, k, k, v, v)


def _gla_kernel(q_ref, k_ref, v_ref, r_ref, g_ref, wg_ref, bg_ref, ng_ref, o_ref, st_ref, *,
                chunks_per_step):
    c = B_CHUNK

    @pl.when(pl.program_id(0) == 0)
    def _():
        st_ref[...] = jnp.zeros_like(st_ref)

    batch = q_ref.shape[0]
    log_as = []
    for b in range(batch):
        gate = _dot(g_ref[b].astype(BF16), wg_ref[...]) + bg_ref[...]
        log_as.append(jax.nn.log_sigmoid(gate) / B_GATE_TAU)
    ri = lax.broadcasted_iota(jnp.int32, (c, c), 0)
    ci = lax.broadcasted_iota(jnp.int32, (c, c), 1)
    tri = (ri >= ci).astype(BF16)
    lane = lax.broadcasted_iota(jnp.int32, (c, LANES), 1)
    low = lane < B_DK
    ri2 = lax.broadcasted_iota(jnp.int32, (2 * c, 2 * c), 0)
    ci2 = lax.broadcasted_iota(jnp.int32, (2 * c, 2 * c), 1)
    causal2 = ((ri2 < c) == (ci2 < c)) & ((ri2 % c) >= (ci2 % c))
    low_sq = ci2 < B_DK
    norm_g = ng_ref[...]
    n_pairs = B_HEADS // 2
    dk2 = B_HEADS * B_DK

    def stack_heads(t):
        zeros = jnp.zeros_like(t)
        return jnp.concatenate([jnp.where(low, t, zeros), jnp.where(low, zeros, t)], axis=0)

    items = [(ch, b) for ch in range(chunks_per_step) for b in range(batch)]
    pending = []
    for ch, b in items:
        rows = slice(ch * c, (ch + 1) * c)
        la = log_as[b][rows]
        hi = la.astype(BF16)
        rem = la - hi.astype(F32)
        mid = rem.astype(BF16)
        lo = (rem - mid.astype(F32)).astype(BF16)
        terms = _dot(tri, jnp.concatenate([hi, mid, lo], axis=1))
        bcum = terms[:, :dk2] + terms[:, dk2:2 * dk2] + terms[:, 2 * dk2:]
        b_last = bcum[c - 1:c, :]
        b_mid = bcum[c // 2 - 1:c // 2, :]
        qf = q_ref[b, rows, :] * (B_DK ** -0.5)
        kf = k_ref[b, rows, :]
        q_in = (qf * jnp.exp(bcum - b_mid)).astype(BF16)
        k_in = (kf * jnp.exp(b_mid - bcum)).astype(BF16)
        q_st = (qf * jnp.exp(bcum)).astype(BF16)
        k_st = (kf * jnp.exp(b_last - bcum)).astype(BF16)
        decay = jnp.exp(b_last)
        for pr in range(n_pairs):
            pc = slice(pr * LANES, (pr + 1) * LANES)
            vcols = slice(2 * pr * B_DV, (2 * pr + 2) * B_DV)
            v_lanes = v_ref[b, rows, vcols].astype(BF16)
            v_rows = jnp.concatenate([v_lanes[:, :B_DV], v_lanes[:, B_DV:]], axis=0)
            att = _dot_nt(stack_heads(q_in[:, pc]), stack_heads(k_in[:, pc]))
            att = jnp.where(causal2, att, 0.0)
            o_intra = _dot(att.astype(BF16), v_rows)
            d_st = _dot_tn(v_lanes, k_st[:, pc])
            d_st = jnp.where(low_sq, d_st[:B_DV], d_st[B_DV:])
            pending.append((b, pr, rows, vcols, o_intra, stack_heads(q_st[:, pc]), decay[:, pc], d_st))

    for b, pr, rows, vcols, o_intra, q_state, decay, d_st in pending:
        st = st_ref[b, pr]
        o = _rms(o_intra + _dot_nt(q_state, st.astype(BF16)), norm_g)
        rg = r_ref[b, rows, vcols]
        gate_r = jax.nn.silu(jnp.concatenate([rg[:, :B_DV], rg[:, B_DV:]], axis=0))
        o = (o * gate_r).astype(o_ref.dtype)
        o_ref[b, rows, 2 * pr * B_DV:(2 * pr + 1) * B_DV] = o[:c]
        o_ref[b, rows, (2 * pr + 1) * B_DV:(2 * pr + 2) * B_DV] = o[c:]
        st_ref[b, pr] = st * decay + d_st


def _gla(qb, kb, vb, rb, gb, w_gate, b_gate, norm_g, batch):
    n = qb.shape[0]
    seq = n // batch
    t = min(512, seq)
    r3 = lambda a: a.reshape(batch, seq, a.shape[1])
    blk = lambda c: pl.BlockSpec((batch, t, c), lambda i: (0, i, 0))
    full = lambda a: pl.BlockSpec(a.shape, lambda i: (0,) * a.ndim)
    dk2 = B_HEADS * B_DK
    dv2 = B_HEADS * B_DV
    out = pl.pallas_call(
        functools.partial(_gla_kernel, chunks_per_step=t // B_CHUNK),
        grid=(seq // t,),
        in_specs=[blk(dk2), blk(dk2), blk(dv2), blk(dv2), blk(LANES),
                  full(w_gate), full(b_gate), full(norm_g)],
        out_specs=blk(dv2),
        out_shape=jax.ShapeDtypeStruct((batch, seq, dv2), BF16),
        scratch_shapes=[pltpu.VMEM((batch, B_HEADS // 2, B_DV, 2 * B_DK), F32)],
        compiler_params=_params(("arbitrary",)),
        name="gla",
    )(r3(qb), r3(kb), r3(vb), r3(rb), r3(gb), w_gate, b_gate, norm_g)
    return out.reshape(n, dv2)


def _proj_ffn_kernel(a_ref, b_ref, wo_ref, x_ref, gmix_ref, gpre_ref, wg_ref, wu_ref, wd_ref,
                     gffn_ref, x_out, *, tf):
    ca = a_ref.shape[1]
    y = _dot(a_ref[...], wo_ref[0:ca, :]) + _dot(b_ref[...], wo_ref[ca:, :])
    x1 = x_ref[...] + _rms(y, gmix_ref[...])
    h = _rms(x1, gpre_ref[...]).astype(BF16)
    f = None
    for c in range(wg_ref.shape[1] // tf):
        cols = slice(c * tf, (c + 1) * tf)
        g = _dot(h, wg_ref[:, cols])
        u = _dot(h, wu_ref[:, cols])
        part = _dot((jax.nn.silu(g) * u).astype(BF16), wd_ref[cols, :])
        f = part if f is None else f + part
    x_out[...] = x1 + _rms(f, gffn_ref[...])


def _proj_ffn(a, b, w_out, x2d, g_mix, g_pre, wg, wu, wd, g_ffn, tm, tf):
    n, d = x2d.shape
    ca, cb = a.shape[1], b.shape[1]
    dff = wg.shape[1]
    row = lambda c: pl.BlockSpec((tm, c), lambda i: (i, 0))
    resident = lambda r, c: pl.BlockSpec((r, c), lambda i: (0, 0), pipeline_mode=pl.Buffered(1))
    vec = lambda g: g.reshape(1, d)
    return pl.pallas_call(
        functools.partial(_proj_ffn_kernel, tf=tf),
        grid=(n // tm,),
        in_specs=[row(ca), row(cb), resident(ca + cb, d), row(d), resident(1, d),
                  resident(1, d), resident(d, dff), resident(d, dff), resident(dff, d), resident(1, d)],
        out_specs=row(d),
        out_shape=jax.ShapeDtypeStruct((n, d), F32),
        compiler_params=_params(("parallel",)),
        name="proj_ffn",
    )(a, b, w_out, x2d, vec(g_mix), vec(g_pre), wg, wu, wd, vec(g_ffn))


def _norm_proj_mla_kernel(x_ref, g_ref, w_ref, qn_ref, kvn_ref, wq_ref, wk_ref, wv_ref,
                          cosq_ref, sinq_ref, cosk_ref, sink_ref,
                          qc_out, kc_out, vc_out, q_out, k_out, v_out, *, splits, scale):
    h = _rms(x_ref[...], g_ref[...]).astype(BF16)
    parts = []
    for start, size, part_scale, _ in splits:
        y = _dot(h, w_ref[:, start:start + size])
        parts.append(y * part_scale if part_scale != 1.0 else y)
    for o_ref, y in zip((qc_out, kc_out, vc_out), parts[:3]):
        for j in range(y.shape[1] // LANES):
            o_ref[j] = y[:, j * LANES:(j + 1) * LANES].astype(o_ref.dtype)
    c_q, c_kv, kr = parts[3:]
    hq = _rms(c_q, qn_ref[...]).astype(BF16)
    hkv = _rms(c_kv, kvn_ref[...]).astype(BF16)
    lane = lax.broadcasted_iota(jnp.int32, (x_ref.shape[0], LANES), 1)
    half = D_ROPE // 2

    def swap_halves(t, start):
        return jnp.where(lane < start + half, pltpu.roll(t, LANES - half, 1), pltpu.roll(t, half, 1))

    kpe = kr * cosk_ref[...] + swap_halves(kr, 0) * sink_ref[...]
    kpe = pltpu.roll(kpe, D_NOPE, 1)
    cosq = cosq_ref[...]
    sinq = sinq_ref[...]
    for h in range(D_HEADS):
        cols = slice(h * LANES, (h + 1) * LANES)
        q = _dot(hq, wq_ref[:, cols])
        q = q * cosq + swap_halves(q, D_NOPE) * sinq
        q_out[:, cols] = (q * scale).astype(q_out.dtype)
        k = _dot(hkv, wk_ref[:, cols])
        k_out[:, cols] = (k + kpe).astype(k_out.dtype)
    for pr in range(D_HEADS // 2):
        v = _dot(hkv, wv_ref[:, pr * LANES:(pr + 1) * LANES])
        v_out[:, 2 * pr * LANES:(2 * pr + 1) * LANES] = v.astype(v_out.dtype)
        v_out[:, (2 * pr + 1) * LANES:(2 * pr + 2) * LANES] = jnp.ones(v.shape, v_out.dtype)


def _norm_proj_mla(x2d, g, w, splits, slab_dtype, q_norm, kv_norm, wq, wk, wv, tables, seq, tm):
    n, d = x2d.shape
    tiles_per_seq = seq // tm
    row = lambda c: pl.BlockSpec((tm, c), lambda i: (i, 0))
    const = lambda a: pl.BlockSpec(a.shape, lambda i: (0,) * a.ndim)
    tab = pl.BlockSpec((tm, LANES), lambda i: (i % tiles_per_seq, 0))
    slab = lambda size: pl.BlockSpec((size // LANES, tm, LANES), lambda i: (0, i, 0))
    slab_shape = lambda size: jax.ShapeDtypeStruct((size // LANES, n, LANES), slab_dtype)
    hw = D_HEADS * LANES
    slab_sizes = [size for _, size, _, _ in splits[:3]]
    g2 = g.reshape(1, d)
    return pl.pallas_call(
        functools.partial(_norm_proj_mla_kernel, splits=splits,
                          scale=float((D_NOPE + D_ROPE) ** -0.5) * LOG2E),
        grid=(n // tm,),
        in_specs=[row(d), const(g2), const(w), const(q_norm), const(kv_norm),
                  const(wq), const(wk), const(wv), tab, tab, tab, tab],
        out_specs=[slab(s) for s in slab_sizes] + [row(hw), row(hw), row(hw)],
        out_shape=[slab_shape(s) for s in slab_sizes] + [jax.ShapeDtypeStruct((n, hw), BF16)] * 3,
        compiler_params=_params(("parallel",)),
        name="norm_proj_mla",
    )(x2d, g2, w, q_norm, kv_norm, wq, wk, wv, *tables)


def _rope_tables(seq):
    half = D_ROPE // 2
    freqs = np.float32(ROPE_BASE) ** (-np.arange(half, dtype=np.float32) / np.float32(half))
    ang = np.arange(seq, dtype=np.float32)[:, None] * freqs[None, :]
    cos, sin = jnp.asarray(np.cos(ang), F32), jnp.asarray(np.sin(ang), F32)
    ones = jnp.ones((seq, D_NOPE), F32)
    zeros = lambda c: jnp.zeros((seq, c), F32)
    pad = LANES - D_NOPE - D_ROPE
    cosq = jnp.concatenate([ones, cos, cos, zeros(pad)], axis=1)
    sinq = jnp.concatenate([zeros(D_NOPE), -sin, sin, zeros(pad)], axis=1)
    cosk = jnp.concatenate([cos, cos, zeros(LANES - D_ROPE)], axis=1)
    sink = jnp.concatenate([-sin, sin, zeros(LANES - D_ROPE)], axis=1)
    return cosq, sinq, cosk, sink


def _mla_attn_kernel(qe_ref, qo_ref, k_ref, v_ref, oe_ref, oo_ref, q_sc, s_ref, p_ref, m_ref, l_ref,
                     a_ref, acc_ref, *, tq, rb, n_full, group):
    tk = 2 * tq
    t_even = pl.program_id(2)
    q_sc[0] = qe_ref[0]
    q_sc[1] = qo_ref[0]
    m_ref[...] = jnp.full(m_ref.shape, NEG_INF, F32)
    l_ref[...] = jnp.zeros(l_ref.shape, F32)
    acc_ref[...] = jnp.zeros(acc_ref.shape, F32)

    diag_odd = (1, n_full - t_even, tk, "odd")
    diag_even = (0, t_even, tq, "even")

    def full_tile(u):
        sel = (u >= t_even).astype(jnp.int32)
        return (sel, u - sel * t_even, tk, None)

    def scores(tile, buf):
        sel, kt, width, mask = tile
        k0 = pl.multiple_of(kt * tk, tk)
        for e in range(2):
            qblk = q_sc[sel, :, e * LANES:(e + 1) * LANES]
            kblk = k_ref[0, pl.ds(k0, width), e * LANES:(e + 1) * LANES]
            s = _dot_nt(qblk, kblk)
            if mask is not None:
                row = lax.broadcasted_iota(jnp.int32, (tq, width), 0)
                col = lax.broadcasted_iota(jnp.int32, (tq, width), 1)
                s = jnp.where(row >= (col if mask == "even" else col - tq), s, NEG_INF)
            s_ref[buf, e, :, 0:width] = s

    def softmax(tile, buf):
        sel, _, width, _ = tile
        for e in range(2):
            for r in range(tq // rb):
                rows = slice(r * rb, (r + 1) * rb)
                m_old = m_ref[sel, e, rows, :]
                m_new = jnp.maximum(m_old, jnp.max(s_ref[buf, e, rows, 0:width], axis=-1, keepdims=True))
                a_ref[buf, e, rows, :] = jnp.exp2(m_old - m_new)
                m_ref[sel, e, rows, :] = m_new
            for r in range(tq // rb):
                rows = slice(r * rb, (r + 1) * rb)
                m_new = jnp.concatenate([m_ref[sel, e, rows, :]] * (width // LANES), axis=1)
                p_ref[buf, e, rows, 0:width] = jnp.exp2(s_ref[buf, e, rows, 0:width] - m_new).astype(BF16)

    def values(tile, buf):
        sel, kt, width, _ = tile
        k0 = pl.multiple_of(kt * tk, tk)
        vblk = v_ref[0, pl.ds(k0, width), :]
        for e in range(2):
            pv = _dot(p_ref[buf, e, :, 0:width], vblk)
            alpha = a_ref[buf, e]
            acc_ref[sel, e] = alpha * acc_ref[sel, e] + pv[:, :LANES]
            l_ref[sel, e] = alpha * l_ref[sel, e] + pv[:, LANES:]

    def step(t, prev, this, nxt):
        if nxt is not None:
            scores(nxt, (t + 1) % 2)
        if prev is not None:
            values(prev, (t - 1) % 2)
        softmax(this, t % 2)

    def static_tile(t):
        return (diag_odd, diag_even)[t] if t < 2 else full_tile(jnp.int32(t - 2))

    n_pos = n_full + 2
    loop_lo = min(3, n_pos)
    loop_iters = max(0, (n_pos - 1 - loop_lo) // group)
    loop_hi = loop_lo + group * loop_iters

    scores(diag_odd, 0)
    for t in range(loop_lo):
        step(t, static_tile(t - 1) if t >= 1 else None, static_tile(t),
             static_tile(t + 1) if t + 1 < n_pos else None)

    def body(jj, carry):
        for t_off in range(group):
            t = loop_lo + t_off
            u = group * jj + (loop_lo - 2) + t_off
            step(t, full_tile(u - 1), full_tile(u), full_tile(u + 1))
        return carry

    lax.fori_loop(0, loop_iters, body, 0)
    for t in range(loop_hi, n_pos):
        step(t, static_tile(t - 1), static_tile(t), static_tile(t + 1) if t + 1 < n_pos else None)
    values(static_tile(n_pos - 1), (n_pos - 1) % 2)

    lane = lax.broadcasted_iota(jnp.int32, (tq, LANES), 1)
    for sel, o_ref in ((0, oe_ref), (1, oo_ref)):
        outs = [acc_ref[sel, e] / l_ref[sel, e] for e in range(2)]
        o_ref[0] = jnp.where(lane < D_V, outs[0], outs[1]).astype(o_ref.dtype)


def _mla_attn(q, k, v, batch, tq):
    n = q.shape[0]
    seq = n // batch
    n_pairs = D_HEADS // 2
    half = seq // tq // 2
    q3 = q.reshape(batch, seq, q.shape[1])
    k3 = k.reshape(batch, seq, k.shape[1])
    v3 = v.reshape(batch, seq, v.shape[1])
    stat = pltpu.VMEM((2, 2, tq, LANES), F32)
    q_spec = lambda im: pl.BlockSpec((1, tq, 2 * LANES), im)
    kv_map = lambda b, p, i: (b, 0, p)
    k_spec = pl.BlockSpec((1, seq, 2 * LANES), kv_map)
    v_spec = pl.BlockSpec((1, seq, 2 * LANES), kv_map, pipeline_mode=pl.Buffered(1))
    o_shape = jax.ShapeDtypeStruct((batch, half * tq, n_pairs * LANES), BF16)
    o_even, o_odd = pl.pallas_call(
        functools.partial(_mla_attn_kernel, tq=tq, rb=16, n_full=half - 1, group=2),
        grid=(batch, n_pairs, half),
        in_specs=[q_spec(lambda b, p, i: (b, 2 * i, p)),
                  q_spec(lambda b, p, i: (b, 2 * half - 1 - 2 * i, p)),
                  k_spec, v_spec],
        out_specs=[pl.BlockSpec((1, tq, LANES), lambda b, p, i: (b, i, p)),
                   pl.BlockSpec((1, tq, LANES), lambda b, p, i: (b, half - 1 - i, p))],
        out_shape=[o_shape, o_shape],
        scratch_shapes=[pltpu.VMEM((2, tq, 2 * LANES), BF16),
                        pltpu.VMEM((2, 2, tq, 2 * tq), F32), pltpu.VMEM((2, 2, tq, 2 * tq), BF16),
                        stat, stat, stat, stat],
        compiler_params=_params(("parallel", "parallel", "arbitrary")),
        name="mla_attn",
    )(q3, q3, k3, v3)
    shape4 = (batch, half, tq, n_pairs * LANES)
    out = jnp.stack([o_even.reshape(shape4), o_odd.reshape(shape4)], axis=2)
    return out.reshape(n, n_pairs * LANES)


def _pad_cols(w, width):
    return jnp.pad(w, ((0, 0), (0, width - w.shape[1])))


def _layer0_w_in(w):
    return _pad_cols(w, 25 * LANES).astype(BF16)


def _layer1_w_in(w):
    hd = HEAD_DIM
    qc = w[:, :C_HEADS * hd]
    o = C_HEADS * hd
    kc = w[:, o:o + C_KV_HEADS * hd]
    o += C_KV_HEADS * hd
    vc = w[:, o:o + C_KV_HEADS * hd]
    o += C_KV_HEADS * hd
    rest = w[:, o:o + D_Q_RANK + D_KV_RANK]
    o += D_Q_RANK + D_KV_RANK
    kr = _pad_cols(w[:, o:o + D_ROPE], LANES)
    rep = lambda t: jnp.concatenate([t[:, g * hd:(g + 1) * hd] for g in range(C_KV_HEADS) for _ in range(2)], axis=1)
    return jnp.concatenate([qc, rep(kc), rep(vc), rest, kr], axis=1).astype(BF16)


def _mla_weights(w_uq, w_ukv):
    dq = D_NOPE + D_ROPE
    wq = w_uq.reshape(D_Q_RANK, D_HEADS, dq)
    wq = jnp.pad(wq, ((0, 0), (0, 0), (0, LANES - dq))).reshape(D_Q_RANK, D_HEADS * LANES)
    wkv = w_ukv.reshape(D_KV_RANK, D_HEADS, D_NOPE + D_V)
    wk = jnp.pad(wkv[:, :, :D_NOPE], ((0, 0), (0, 0), (0, LANES - D_NOPE)))
    wk = wk.reshape(D_KV_RANK, D_HEADS * LANES)
    wv = wkv[:, :, D_NOPE:].reshape(D_KV_RANK, D_HEADS * D_V)
    return wq.astype(BF16), wk.astype(BF16), wv.astype(BF16)


def _row_tile(n, want):
    return min(want, n)


def _splits(sizes, scales, slabs):
    starts = np.concatenate([[0], np.cumsum(sizes)[:-1]])
    return tuple((int(s), int(z), float(c), bool(b)) for s, z, c, b in zip(starts, sizes, scales, slabs))


def _mix_even(x2d, batch, g_pre, w_in, gla_w_gate_up, gla_b_gate, gla_norm):
    n = x2d.shape[0]
    ah = A_HEADS * HEAD_DIM
    dk2, dv2 = B_HEADS * B_DK, B_HEADS * B_DV
    sizes = (ah, ah, ah, dk2, dk2, dv2, dv2, LANES)
    scales = (HEAD_DIM ** -0.5 * LOG2E,) + (1.0,) * 7
    slabs = (True, True, True) + (False,) * 5
    qa, ka, va, qb, kb, vb, rb, gb = _norm_proj(x2d, g_pre, _layer0_w_in(w_in),
                                                _splits(sizes, scales, slabs),
                                                (F32, F32, F32, F32, F32, BF16, F32, F32),
                                                _row_tile(n, 512))
    o_a = _band_attn(qa, ka, va, batch=batch,
                     patterns=tuple(sorted(((w // d, d) for w, d in A_PATTERNS), key=lambda p: -p[1])),
                     slopes=jnp.asarray(_alibi_slopes(A_HEADS) * np.float32(LOG2E)),
                     m0=jnp.full((A_HEADS,), NEG_INF, F32), l0=jnp.zeros((A_HEADS,), F32))
    w_gate = jnp.pad(gla_w_gate_up, ((0, LANES - B_GATE_RANK), (0, 0))).astype(BF16)
    o_b = _gla(qb, kb, vb, rb, gb, w_gate, gla_b_gate.reshape(1, dk2), gla_norm.reshape(1, B_DV), batch)
    return o_a, o_b


def _mix_odd(x2d, batch, g_pre, w_in, sinks, q_norm, w_uq, kv_norm, w_ukv):
    n = x2d.shape[0]
    seq = n // batch
    ch = C_HEADS * HEAD_DIM
    ckv = 2 * C_KV_HEADS * HEAD_DIM
    sizes = (ch, ckv, ckv, D_Q_RANK, D_KV_RANK, LANES)
    scales = (HEAD_DIM ** -0.5 * LOG2E,) + (1.0,) * 5
    slabs = (True, True, True, False, False, False)
    wq, wk, wv = _mla_weights(w_uq, w_ukv)
    qc, kc, vc, q, k, v = _norm_proj_mla(x2d, g_pre, _layer1_w_in(w_in), _splits(sizes, scales, slabs),
                                         BF16, q_norm.reshape(1, D_Q_RANK), kv_norm.reshape(1, D_KV_RANK),
                                         wq, wk, wv, _rope_tables(seq), seq, _row_tile(seq, 512))
    o_c = _band_attn(qc, kc, vc, batch=batch, patterns=((C_WINDOW - 1, 1),),
                     slopes=jnp.asarray(_alibi_slopes(C_HEADS) * np.float32(LOG2E)),
                     m0=sinks.astype(F32) * LOG2E, l0=jnp.ones((C_HEADS,), F32))
    o_d = _mla_attn(q, k, v, batch, _row_tile(seq // 2, 512))
    return o_c, o_d


def kernel(x, norm_mix_pre, norm_mix_post, norm_ffn_pre, norm_ffn_post, ffn_w_gate, ffn_w_up, ffn_w_down, ab_w_in, ab_w_out, gla_w_gate_up, gla_b_gate, gla_norm, cd_w_in, cd_w_out, swa_sinks, mla_q_norm, mla_w_uq, mla_kv_norm, mla_w_ukv):
    batch, seq, d = x.shape
    n = batch * seq
    x2d = x.reshape(n, d)
    depth = norm_mix_pre.shape[0]
    for layer in range(depth):
        i = layer // 2
        if layer % 2 == 0:
            o1, o2 = _mix_even(x2d, batch, norm_mix_pre[layer], ab_w_in[i], gla_w_gate_up[i],
                               gla_b_gate[i], gla_norm[i])
            w_out = ab_w_out[i].astype(BF16)
        else:
            o1, o2 = _mix_odd(x2d, batch, norm_mix_pre[layer], cd_w_in[i], swa_sinks[i],
                              mla_q_norm[i], mla_w_uq[i], mla_kv_norm[i], mla_w_ukv[i])
            w_out = cd_w_out[i].astype(BF16)
        x2d = _proj_ffn(o1, o2, w_out, x2d, norm_mix_post[layer],
                        norm_ffn_pre[layer], ffn_w_gate[layer].astype(BF16),
                        ffn_w_up[layer].astype(BF16), ffn_w_down[layer].astype(BF16),
                        norm_ffn_post[layer], _row_tile(n, 512), 256)
    return x2d.reshape(batch, seq, d)
```

```python
import functools

import numpy as np
import jax
import jax.numpy as jnp
from jax import lax
from jax.experimental import pallas as pl
from jax.experimental.pallas import tpu as pltpu

F32 = jnp.float32
BF16 = jnp.bfloat16

LANES = 128
V7X_VMEM_BYTES = 64 * 1024 * 1024
VMEM_LIMIT = 56 * 1024 * 1024
LOG2E = float(np.log2(np.e))

D_MODEL = 1024
HEAD_DIM = 64
Q_BLOCK = 128
EPS = 1e-6
NEG_INF = -1e30
A_HEADS = 8
A_PATTERNS = ((128, 1), (512, 4), (2048, 16))
B_HEADS = 4
B_DK = 64
B_DV = 128
B_GATE_RANK = 16
B_GATE_TAU = 16.0
B_CHUNK = 64
C_HEADS = 8
C_KV_HEADS = 2
C_WINDOW = 128
D_HEADS = 8
D_Q_RANK = 384
D_KV_RANK = 256
D_NOPE = 64
D_ROPE = 32
D_V = 64
ROPE_BASE = 10000.0
D_FF = 2816


def _alibi_slopes(n):
    return np.array([2.0 ** (-8.0 * (i + 1) / n) for i in range(n)], dtype=np.float32)


def _params(semantics):
    return pltpu.CompilerParams(dimension_semantics=semantics, vmem_limit_bytes=VMEM_LIMIT)


def _rms(x, g):
    return x * lax.rsqrt(jnp.mean(x * x, axis=-1, keepdims=True) + EPS) * g


def _dot(a, b):
    return jnp.dot(a, b, preferred_element_type=F32)


def _dot_nt(a, b):
    return lax.dot_general(a, b, (((1,), (1,)), ((), ())), preferred_element_type=F32)


def _dot_tn(a, b):
    return lax.dot_general(a, b, (((0,), (0,)), ((), ())), preferred_element_type=F32)


def _norm_proj_kernel(x_ref, g_ref, w_ref, *out_refs, splits):
    h = _rms(x_ref[...], g_ref[...]).astype(BF16)
    for o_ref, (start, size, scale, slabs) in zip(out_refs, splits):
        y = _dot(h, w_ref[:, start:start + size])
        if scale != 1.0:
            y = y * scale
        if slabs:
            for j in range(size // LANES):
                o_ref[j] = y[:, j * LANES:(j + 1) * LANES].astype(o_ref.dtype)
        else:
            o_ref[...] = y.astype(o_ref.dtype)


def _norm_proj(x2d, g, w, splits, dtypes, tm):
    n, d = x2d.shape
    wn = w.shape[1]
    out_shape, out_specs = [], []
    for (_, size, _, slabs), dt in zip(splits, dtypes):
        if slabs:
            out_shape.append(jax.ShapeDtypeStruct((size // LANES, n, LANES), dt))
            out_specs.append(pl.BlockSpec((size // LANES, tm, LANES), lambda i: (0, i, 0)))
        else:
            out_shape.append(jax.ShapeDtypeStruct((n, size), dt))
            out_specs.append(pl.BlockSpec((tm, size), lambda i: (i, 0)))
    return pl.pallas_call(
        functools.partial(_norm_proj_kernel, splits=splits),
        grid=(n // tm,),
        in_specs=[pl.BlockSpec((tm, d), lambda i: (i, 0)),
                  pl.BlockSpec((1, d), lambda i: (0, 0)),
                  pl.BlockSpec((d, wn), lambda i: (0, 0))],
        out_specs=out_specs,
        out_shape=out_shape,
        compiler_params=_params(("parallel",)),
        name="norm_proj",
    )(x2d, g.reshape(1, d), w)


def _band_attn_kernel(slope_ref, m0_ref, l0_ref, q_ref, kc_ref, kp_ref, vc_ref, vp_ref, o_ref,
                      kcat, vcat, acc_ref, m_ref, l_ref, bias_ref, *, patterns, tt):
    hp = pl.program_id(1)
    first_tile = pl.program_id(2) == 0
    kcat[0:tt, :] = kp_ref[0]
    kcat[tt:, :] = kc_ref[0]
    vcat[0:tt, :] = vp_ref[0]
    vcat[tt:, :] = vc_ref[0]
    m_init = [jnp.full((Q_BLOCK, LANES), m0_ref[2 * hp + e], F32) for e in range(2)]
    l_init = [jnp.full((Q_BLOCK, LANES), l0_ref[2 * hp + e], F32) for e in range(2)]

    row = lax.broadcasted_iota(jnp.int32, (Q_BLOCK, 2 * Q_BLOCK), 0)
    col = lax.broadcasted_iota(jnp.int32, (Q_BLOCK, 2 * Q_BLOCK), 1)
    rel = row + Q_BLOCK - col
    lane = lax.broadcasted_iota(jnp.int32, (Q_BLOCK, LANES), 1)
    low = lane < HEAD_DIM
    ones = jnp.ones((2 * Q_BLOCK, LANES), BF16)

    for pi, (max_rel, d) in enumerate(patterns):
        first = pi == 0
        fused_out = pi == len(patterns) - 1 and d == 1
        band = (rel >= 0) & (rel <= max_rel)
        dist = (rel * d).astype(F32)
        for e in range(2):
            bias_ref[e] = slope_ref[2 * hp + e] * dist
        blocks_per_residue = tt // (Q_BLOCK * d)

        for idx in range(tt // Q_BLOCK):
            r = idx // blocks_per_residue
            t = idx % blocks_per_residue
            q0 = r + t * (Q_BLOCK * d)
            k0 = tt - Q_BLOCK * d + q0
            qrows = pl.ds(q0, Q_BLOCK, stride=d) if d > 1 else pl.ds(q0, Q_BLOCK)
            krows = pl.ds(k0, 2 * Q_BLOCK, stride=d) if d > 1 else pl.ds(k0, 2 * Q_BLOCK)
            valid = (band & ((col >= Q_BLOCK) | jnp.logical_not(first_tile))) if t == 0 else band
            qblk = q_ref[0, qrows, :].astype(BF16)
            kblk = kcat[krows, :].astype(BF16)
            vext = jnp.concatenate([vcat[krows, :].astype(BF16), ones], axis=1)
            alphas, pvs, ls = [], [], []
            for e in range(2):
                qm = jnp.where(low if e == 0 else jnp.logical_not(low), qblk, jnp.zeros_like(qblk))
                s = _dot_nt(qm, kblk)
                s = jnp.where(valid, s - bias_ref[e], NEG_INF)
                m_old = m_init[e] if first else m_ref[e, qrows, :]
                l_old = l_init[e] if first else l_ref[e, qrows, :]
                m_new = jnp.maximum(m_old, jnp.max(s, axis=-1, keepdims=True))
                alpha = jnp.exp2(m_old - m_new)
                p = jnp.exp2(s - jnp.concatenate([m_new, m_new], axis=1))
                pv = _dot(p.astype(BF16), vext)
                l_new = alpha * l_old + pv[:, LANES:]
                if not fused_out:
                    m_ref[e, qrows, :] = m_new
                    l_ref[e, qrows, :] = l_new
                alphas.append(alpha)
                pvs.append(pv[:, :LANES])
                ls.append(l_new)
            acc = jnp.where(low, pvs[0], pvs[1])
            if not first:
                acc = jnp.where(low, alphas[0], alphas[1]) * acc_ref[qrows, :] + acc
            if fused_out:
                o_ref[pl.ds(q0, Q_BLOCK), :] = (acc / jnp.where(low, ls[0], ls[1])).astype(o_ref.dtype)
            else:
                acc_ref[qrows, :] = acc

    if patterns[-1][1] != 1:
        lane_t = lax.broadcasted_iota(jnp.int32, (tt, LANES), 1)
        o_ref[...] = (acc_ref[...] / jnp.where(lane_t < HEAD_DIM, l_ref[0], l_ref[1])).astype(o_ref.dtype)


def _band_attn(q, k, v, *, batch, patterns, slopes, m0, l0):
    n_pairs, n, _ = q.shape
    n_kv = k.shape[0]
    seq = n // batch
    max_d = max(d for _, d in patterns)
    assert max_d == 1 or (q.dtype == F32 and k.dtype == F32 and v.dtype == F32)
    tt = min(seq, max(2048, Q_BLOCK * max_d))
    tiles = seq // tt
    cur = lambda b, p, i: (p, b * tiles + i, 0)
    cur_kv = lambda b, p, i: (p * n_kv // n_pairs, b * tiles + i, 0)
    prev_kv = lambda b, p, i: (p * n_kv // n_pairs, b * tiles + jnp.maximum(i - 1, 0), 0)
    smem = pl.BlockSpec(memory_space=pltpu.SMEM)
    blk = lambda im: pl.BlockSpec((1, tt, LANES), im)
    stat = pltpu.VMEM((2, tt, LANES), F32)
    return pl.pallas_call(
        functools.partial(_band_attn_kernel, patterns=patterns, tt=tt),
        grid=(batch, n_pairs, tiles),
        in_specs=[smem, smem, smem, blk(cur), blk(cur_kv), blk(prev_kv), blk(cur_kv), blk(prev_kv)],
        out_specs=pl.BlockSpec((tt, LANES), lambda b, p, i: (b * tiles + i, p)),
        out_shape=jax.ShapeDtypeStruct((n, n_pairs * LANES), BF16),
        scratch_shapes=[pltpu.VMEM((2 * tt, LANES), k.dtype), pltpu.VMEM((2 * tt, LANES), v.dtype),
                        pltpu.VMEM((tt, LANES), F32), stat, stat,
                        pltpu.VMEM((2, Q_BLOCK, 2 * Q_BLOCK), F32)],
        compiler_params=_params(("parallel", "parallel", "arbitrary")),
        name="band_attn_p%d" % len(patterns),
    )(slopes, m0, l0, q, k, k, v, v)


def _gla_kernel(q_ref, k_ref, v_ref, r_ref, g_ref, wg_ref, bg_ref, ng_ref, o_ref, st_ref, *,
                chunks_per_step):
    c = B_CHUNK

    @pl.when(pl.program_id(0) == 0)
    def _():
        st_ref[...] = jnp.zeros_like(st_ref)

    batch = q_ref.shape[0]
    log_as = []
    for b in range(batch):
        gate = _dot(g_ref[b].astype(BF16), wg_ref[...]) + bg_ref[...]
        log_as.append(jax.nn.log_sigmoid(gate) / B_GATE_TAU)
    ri = lax.broadcasted_iota(jnp.int32, (c, c), 0)
    ci = lax.broadcasted_iota(jnp.int32, (c, c), 1)
    tri = (ri >= ci).astype(BF16)
    lane = lax.broadcasted_iota(jnp.int32, (c, LANES), 1)
    low = lane < B_DK
    ri2 = lax.broadcasted_iota(jnp.int32, (2 * c, 2 * c), 0)
    ci2 = lax.broadcasted_iota(jnp.int32, (2 * c, 2 * c), 1)
    causal2 = ((ri2 < c) == (ci2 < c)) & ((ri2 % c) >= (ci2 % c))
    low_sq = ci2 < B_DK
    norm_g = ng_ref[...]
    n_pairs = B_HEADS // 2
    dk2 = B_HEADS * B_DK

    def stack_heads(t):
        zeros = jnp.zeros_like(t)
        return jnp.concatenate([jnp.where(low, t, zeros), jnp.where(low, zeros, t)], axis=0)

    items = [(ch, b) for ch in range(chunks_per_step) for b in range(batch)]
    pending = []
    for ch, b in items:
        rows = slice(ch * c, (ch + 1) * c)
        la = log_as[b][rows]
        hi = la.astype(BF16)
        rem = la - hi.astype(F32)
        mid = rem.astype(BF16)
        lo = (rem - mid.astype(F32)).astype(BF16)
        terms = _dot(tri, jnp.concatenate([hi, mid, lo], axis=1))
        bcum = terms[:, :dk2] + terms[:, dk2:2 * dk2] + terms[:, 2 * dk2:]
        b_last = bcum[c - 1:c, :]
        b_mid = bcum[c // 2 - 1:c // 2, :]
        qf = q_ref[b, rows, :] * (B_DK ** -0.5)
        kf = k_ref[b, rows, :]
        q_in = (qf * jnp.exp(bcum - b_mid)).astype(BF16)
        k_in = (kf * jnp.exp(b_mid - bcum)).astype(BF16)
        q_st = (qf * jnp.exp(bcum)).astype(BF16)
        k_st = (kf * jnp.exp(b_last - bcum)).astype(BF16)
        decay = jnp.exp(b_last)
        for pr in range(n_pairs):
            pc = slice(pr * LANES, (pr + 1) * LANES)
            vcols = slice(2 * pr * B_DV, (2 * pr + 2) * B_DV)
            v_lanes = v_ref[b, rows, vcols].astype(BF16)
            v_rows = jnp.concatenate([v_lanes[:, :B_DV], v_lanes[:, B_DV:]], axis=0)
            att = _dot_nt(stack_heads(q_in[:, pc]), stack_heads(k_in[:, pc]))
            att = jnp.where(causal2, att, 0.0)
            o_intra = _dot(att.astype(BF16), v_rows)
            d_st = _dot_tn(v_lanes, k_st[:, pc])
            d_st = jnp.where(low_sq, d_st[:B_DV], d_st[B_DV:])
            pending.append((b, pr, rows, vcols, o_intra, stack_heads(q_st[:, pc]), decay[:, pc], d_st))

    for b, pr, rows, vcols, o_intra, q_state, decay, d_st in pending:
        st = st_ref[b, pr]
        o = _rms(o_intra + _dot_nt(q_state, st.astype(BF16)), norm_g)
        rg = r_ref[b, rows, vcols]
        gate_r = jax.nn.silu(jnp.concatenate([rg[:, :B_DV], rg[:, B_DV:]], axis=0))
        o = (o * gate_r).astype(o_ref.dtype)
        o_ref[b, rows, 2 * pr * B_DV:(2 * pr + 1) * B_DV] = o[:c]
        o_ref[b, rows, (2 * pr + 1) * B_DV:(2 * pr + 2) * B_DV] = o[c:]
        st_ref[b, pr] = st * decay + d_st


def _gla(qb, kb, vb, rb, gb, w_gate, b_gate, norm_g, batch):
    n = qb.shape[0]
    seq = n // batch
    t = min(512, seq)
    r3 = lambda a: a.reshape(batch, seq, a.shape[1])
    blk = lambda c: pl.BlockSpec((batch, t, c), lambda i: (0, i, 0))
    full = lambda a: pl.BlockSpec(a.shape, lambda i: (0,) * a.ndim)
    dk2 = B_HEADS * B_DK
    dv2 = B_HEADS * B_DV
    out = pl.pallas_call(
        functools.partial(_gla_kernel, chunks_per_step=t // B_CHUNK),
        grid=(seq // t,),
        in_specs=[blk(dk2), blk(dk2), blk(dv2), blk(dv2), blk(LANES),
                  full(w_gate), full(b_gate), full(norm_g)],
        out_specs=blk(dv2),
        out_shape=jax.ShapeDtypeStruct((batch, seq, dv2), BF16),
        scratch_shapes=[pltpu.VMEM((batch, B_HEADS // 2, B_DV, 2 * B_DK), F32)],
        compiler_params=_params(("arbitrary",)),
        name="gla",
    )(r3(qb), r3(kb), r3(vb), r3(rb), r3(gb), w_gate, b_gate, norm_g)
    return out.reshape(n, dv2)


def _proj_ffn_kernel(a_ref, b_ref, wo_ref, x_ref, gmix_ref, gpre_ref, wg_ref, wu_ref, wd_ref,
                     gffn_ref, x_out, *, tf, sub_rows):
    ca = a_ref.shape[1]
    for r0 in range(0, x_ref.shape[0], sub_rows):
        rows = slice(r0, r0 + sub_rows)
        y = _dot(a_ref[rows, :], wo_ref[0:ca, :]) + _dot(b_ref[rows, :], wo_ref[ca:, :])
        x1 = x_ref[rows, :] + _rms(y, gmix_ref[...])
        h = _rms(x1, gpre_ref[...]).astype(BF16)
        f = None
        for c in range(wg_ref.shape[1] // tf):
            cols = slice(c * tf, (c + 1) * tf)
            g = _dot(h, wg_ref[:, cols])
            u = _dot(h, wu_ref[:, cols])
            part = _dot((jax.nn.silu(g) * u).astype(BF16), wd_ref[cols, :])
            f = part if f is None else f + part
        x_out[rows, :] = x1 + _rms(f, gffn_ref[...])


def _proj_ffn(a, b, w_out, x2d, g_mix, g_pre, wg, wu, wd, g_ffn, tm, tf):
    n, d = x2d.shape
    ca, cb = a.shape[1], b.shape[1]
    dff = wg.shape[1]
    row = lambda c: pl.BlockSpec((tm, c), lambda i: (i, 0))
    resident = lambda r, c: pl.BlockSpec((r, c), lambda i: (0, 0), pipeline_mode=pl.Buffered(1))
    vec = lambda g: g.reshape(1, d)
    return pl.pallas_call(
        functools.partial(_proj_ffn_kernel, tf=tf, sub_rows=min(tm, 512)),
        grid=(n // tm,),
        in_specs=[row(ca), row(cb), resident(ca + cb, d), row(d), resident(1, d),
                  resident(1, d), resident(d, dff), resident(d, dff), resident(dff, d), resident(1, d)],
        out_specs=row(d),
        out_shape=jax.ShapeDtypeStruct((n, d), F32),
        compiler_params=_params(("parallel",)),
        name="proj_ffn",
    )(a, b, w_out, x2d, vec(g_mix), vec(g_pre), wg, wu, wd, vec(g_ffn))


def _norm_proj_mla_kernel(x_ref, g_ref, w_ref, qn_ref, kvn_ref, wq_ref, wk_ref, wv_ref,
                          cosq_ref, sinq_ref, cosk_ref, sink_ref,
                          qc_out, kc_out, vc_out, q_out, k_out, v_out, *, splits, scale):
    h = _rms(x_ref[...], g_ref[...]).astype(BF16)
    parts = []
    for start, size, part_scale, _ in splits:
        y = _dot(h, w_ref[:, start:start + size])
        parts.append(y * part_scale if part_scale != 1.0 else y)
    for o_ref, y in zip((qc_out, kc_out, vc_out), parts[:3]):
        for j in range(y.shape[1] // LANES):
            o_ref[j] = y[:, j * LANES:(j + 1) * LANES].astype(o_ref.dtype)
    c_q, c_kv, kr = parts[3:]
    hq = _rms(c_q, qn_ref[...]).astype(BF16)
    hkv = _rms(c_kv, kvn_ref[...]).astype(BF16)
    lane = lax.broadcasted_iota(jnp.int32, (x_ref.shape[0], LANES), 1)
    half = D_ROPE // 2

    def swap_halves(t, start):
        return jnp.where(lane < start + half, pltpu.roll(t, LANES - half, 1), pltpu.roll(t, half, 1))

    kpe = kr * cosk_ref[...] + swap_halves(kr, 0) * sink_ref[...]
    kpe = pltpu.roll(kpe, D_NOPE, 1)
    cosq = cosq_ref[...]
    sinq = sinq_ref[...]
    for h in range(D_HEADS):
        cols = slice(h * LANES, (h + 1) * LANES)
        q = _dot(hq, wq_ref[:, cols])
        q = q * cosq + swap_halves(q, D_NOPE) * sinq
        q_out[:, cols] = (q * scale).astype(q_out.dtype)
        k = _dot(hkv, wk_ref[:, cols])
        k_out[:, cols] = (k + kpe).astype(k_out.dtype)
    for pr in range(D_HEADS // 2):
        v = _dot(hkv, wv_ref[:, pr * LANES:(pr + 1) * LANES])
        v_out[:, 2 * pr * LANES:(2 * pr + 1) * LANES] = v.astype(v_out.dtype)
        v_out[:, (2 * pr + 1) * LANES:(2 * pr + 2) * LANES] = jnp.ones(v.shape, v_out.dtype)


def _norm_proj_mla(x2d, g, w, splits, slab_dtype, q_norm, kv_norm, wq, wk, wv, tables, seq, tm):
    n, d = x2d.shape
    tiles_per_seq = seq // tm
    row = lambda c: pl.BlockSpec((tm, c), lambda i: (i, 0))
    const = lambda a: pl.BlockSpec(a.shape, lambda i: (0,) * a.ndim)
    tab = pl.BlockSpec((tm, LANES), lambda i: (i % tiles_per_seq, 0))
    slab = lambda size: pl.BlockSpec((size // LANES, tm, LANES), lambda i: (0, i, 0))
    slab_shape = lambda size: jax.ShapeDtypeStruct((size // LANES, n, LANES), slab_dtype)
    hw = D_HEADS * LANES
    slab_sizes = [size for _, size, _, _ in splits[:3]]
    g2 = g.reshape(1, d)
    return pl.pallas_call(
        functools.partial(_norm_proj_mla_kernel, splits=splits,
                          scale=float((D_NOPE + D_ROPE) ** -0.5) * LOG2E),
        grid=(n // tm,),
        in_specs=[row(d), const(g2), const(w), const(q_norm), const(kv_norm),
                  const(wq), const(wk), const(wv), tab, tab, tab, tab],
        out_specs=[slab(s) for s in slab_sizes] + [row(hw), row(hw), row(hw)],
        out_shape=[slab_shape(s) for s in slab_sizes] + [jax.ShapeDtypeStruct((n, hw), BF16)] * 3,
        compiler_params=_params(("parallel",)),
        name="norm_proj_mla",
    )(x2d, g2, w, q_norm, kv_norm, wq, wk, wv, *tables)


def _rope_tables(seq):
    half = D_ROPE // 2
    freqs = np.float32(ROPE_BASE) ** (-np.arange(half, dtype=np.float32) / np.float32(half))
    ang = np.arange(seq, dtype=np.float32)[:, None] * freqs[None, :]
    cos, sin = jnp.asarray(np.cos(ang), F32), jnp.asarray(np.sin(ang), F32)
    ones = jnp.ones((seq, D_NOPE), F32)
    zeros = lambda c: jnp.zeros((seq, c), F32)
    pad = LANES - D_NOPE - D_ROPE
    cosq = jnp.concatenate([ones, cos, cos, zeros(pad)], axis=1)
    sinq = jnp.concatenate([zeros(D_NOPE), -sin, sin, zeros(pad)], axis=1)
    cosk = jnp.concatenate([cos, cos, zeros(LANES - D_ROPE)], axis=1)
    sink = jnp.concatenate([-sin, sin, zeros(LANES - D_ROPE)], axis=1)
    return cosq, sinq, cosk, sink


def _mla_attn_kernel(qe_ref, qo_ref, k_ref, v_ref, oe_ref, oo_ref, q_sc, s_ref, p_ref, m_ref, l_ref,
                     a_ref, acc_ref, *, tq, rb, n_full, group):
    tk = 2 * tq
    t_even = pl.program_id(2)
    q_sc[0] = qe_ref[0]
    q_sc[1] = qo_ref[0]
    m_ref[...] = jnp.full(m_ref.shape, NEG_INF, F32)
    l_ref[...] = jnp.zeros(l_ref.shape, F32)
    acc_ref[...] = jnp.zeros(acc_ref.shape, F32)

    diag_odd = (1, n_full - t_even, tk, "odd")
    diag_even = (0, t_even, tq, "even")

    def full_tile(u):
        sel = (u >= t_even).astype(jnp.int32)
        return (sel, u - sel * t_even, tk, None)

    def scores(tile, buf):
        sel, kt, width, mask = tile
        k0 = pl.multiple_of(kt * tk, tk)
        for e in range(2):
            qblk = q_sc[sel, :, e * LANES:(e + 1) * LANES]
            kblk = k_ref[0, pl.ds(k0, width), e * LANES:(e + 1) * LANES]
            s = _dot_nt(qblk, kblk)
            if mask is not None:
                row = lax.broadcasted_iota(jnp.int32, (tq, width), 0)
                col = lax.broadcasted_iota(jnp.int32, (tq, width), 1)
                s = jnp.where(row >= (col if mask == "even" else col - tq), s, NEG_INF)
            s_ref[buf, e, :, 0:width] = s

    def softmax(tile, buf):
        sel, _, width, _ = tile
        for e in range(2):
            for r in range(tq // rb):
                rows = slice(r * rb, (r + 1) * rb)
                m_old = m_ref[sel, e, rows, :]
                m_new = jnp.maximum(m_old, jnp.max(s_ref[buf, e, rows, 0:width], axis=-1, keepdims=True))
                a_ref[buf, e, rows, :] = jnp.exp2(m_old - m_new)
                m_ref[sel, e, rows, :] = m_new
            for r in range(tq // rb):
                rows = slice(r * rb, (r + 1) * rb)
                m_new = jnp.concatenate([m_ref[sel, e, rows, :]] * (width // LANES), axis=1)
                p_ref[buf, e, rows, 0:width] = jnp.exp2(s_ref[buf, e, rows, 0:width] - m_new).astype(BF16)

    def values(tile, buf):
        sel, kt, width, _ = tile
        k0 = pl.multiple_of(kt * tk, tk)
        vblk = v_ref[0, pl.ds(k0, width), :]
        for e in range(2):
            pv = _dot(p_ref[buf, e, :, 0:width], vblk)
            alpha = a_ref[buf, e]
            acc_ref[sel, e] = alpha * acc_ref[sel, e] + pv[:, :LANES]
            l_ref[sel, e] = alpha * l_ref[sel, e] + pv[:, LANES:]

    def step(t, prev, this, nxt):
        if nxt is not None:
            scores(nxt, (t + 1) % 2)
        if prev is not None:
            values(prev, (t - 1) % 2)
        softmax(this, t % 2)

    def static_tile(t):
        return (diag_odd, diag_even)[t] if t < 2 else full_tile(jnp.int32(t - 2))

    n_pos = n_full + 2
    loop_lo = min(3, n_pos)
    loop_iters = max(0, (n_pos - 1 - loop_lo) // group)
    loop_hi = loop_lo + group * loop_iters

    scores(diag_odd, 0)
    for t in range(loop_lo):
        step(t, static_tile(t - 1) if t >= 1 else None, static_tile(t),
             static_tile(t + 1) if t + 1 < n_pos else None)

    def body(jj, carry):
        for t_off in range(group):
            t = loop_lo + t_off
            u = group * jj + (loop_lo - 2) + t_off
            step(t, full_tile(u - 1), full_tile(u), full_tile(u + 1))
        return carry

    lax.fori_loop(0, loop_iters, body, 0)
    for t in range(loop_hi, n_pos):
        step(t, static_tile(t - 1), static_tile(t), static_tile(t + 1) if t + 1 < n_pos else None)
    values(static_tile(n_pos - 1), (n_pos - 1) % 2)

    lane = lax.broadcasted_iota(jnp.int32, (tq, LANES), 1)
    for sel, o_ref in ((0, oe_ref), (1, oo_ref)):
        outs = [acc_ref[sel, e] / l_ref[sel, e] for e in range(2)]
        o_ref[0] = jnp.where(lane < D_V, outs[0], outs[1]).astype(o_ref.dtype)


def _mla_attn(q, k, v, batch, tq):
    n = q.shape[0]
    seq = n // batch
    n_pairs = D_HEADS // 2
    half = seq // tq // 2
    q3 = q.reshape(batch, seq, q.shape[1])
    k3 = k.reshape(batch, seq, k.shape[1])
    v3 = v.reshape(batch, seq, v.shape[1])
    stat = pltpu.VMEM((2, 2, tq, LANES), F32)
    q_spec = lambda im: pl.BlockSpec((1, tq, 2 * LANES), im)
    kv_map = lambda b, p, i: (b, 0, p)
    k_spec = pl.BlockSpec((1, seq, 2 * LANES), kv_map)
    v_spec = pl.BlockSpec((1, seq, 2 * LANES), kv_map, pipeline_mode=pl.Buffered(1))
    o_shape = jax.ShapeDtypeStruct((batch, half * tq, n_pairs * LANES), BF16)
    o_even, o_odd = pl.pallas_call(
        functools.partial(_mla_attn_kernel, tq=tq, rb=16, n_full=half - 1, group=2),
        grid=(batch, n_pairs, half),
        in_specs=[q_spec(lambda b, p, i: (b, 2 * i, p)),
                  q_spec(lambda b, p, i: (b, 2 * half - 1 - 2 * i, p)),
                  k_spec, v_spec],
        out_specs=[pl.BlockSpec((1, tq, LANES), lambda b, p, i: (b, i, p)),
                   pl.BlockSpec((1, tq, LANES), lambda b, p, i: (b, half - 1 - i, p))],
        out_shape=[o_shape, o_shape],
        scratch_shapes=[pltpu.VMEM((2, tq, 2 * LANES), BF16),
                        pltpu.VMEM((2, 2, tq, 2 * tq), F32), pltpu.VMEM((2, 2, tq, 2 * tq), BF16),
                        stat, stat, stat, stat],
        compiler_params=_params(("parallel", "parallel", "arbitrary")),
        name="mla_attn",
    )(q3, q3, k3, v3)
    shape4 = (batch, half, tq, n_pairs * LANES)
    out = jnp.stack([o_even.reshape(shape4), o_odd.reshape(shape4)], axis=2)
    return out.reshape(n, n_pairs * LANES)


def _pad_cols(w, width):
    return jnp.pad(w, ((0, 0), (0, width - w.shape[1])))


def _layer0_w_in(w):
    return _pad_cols(w, 25 * LANES).astype(BF16)


def _layer1_w_in(w):
    hd = HEAD_DIM
    qc = w[:, :C_HEADS * hd]
    o = C_HEADS * hd
    kc = w[:, o:o + C_KV_HEADS * hd]
    o += C_KV_HEADS * hd
    vc = w[:, o:o + C_KV_HEADS * hd]
    o += C_KV_HEADS * hd
    rest = w[:, o:o + D_Q_RANK + D_KV_RANK]
    o += D_Q_RANK + D_KV_RANK
    kr = _pad_cols(w[:, o:o + D_ROPE], LANES)
    rep = lambda t: jnp.concatenate([t[:, g * hd:(g + 1) * hd] for g in range(C_KV_HEADS) for _ in range(2)], axis=1)
    return jnp.concatenate([qc, rep(kc), rep(vc), rest, kr], axis=1).astype(BF16)


def _mla_weights(w_uq, w_ukv):
    dq = D_NOPE + D_ROPE
    wq = w_uq.reshape(D_Q_RANK, D_HEADS, dq)
    wq = jnp.pad(wq, ((0, 0), (0, 0), (0, LANES - dq))).reshape(D_Q_RANK, D_HEADS * LANES)
    wkv = w_ukv.reshape(D_KV_RANK, D_HEADS, D_NOPE + D_V)
    wk = jnp.pad(wkv[:, :, :D_NOPE], ((0, 0), (0, 0), (0, LANES - D_NOPE)))
    wk = wk.reshape(D_KV_RANK, D_HEADS * LANES)
    wv = wkv[:, :, D_NOPE:].reshape(D_KV_RANK, D_HEADS * D_V)
    return wq.astype(BF16), wk.astype(BF16), wv.astype(BF16)


def _row_tile(n, want):
    return min(want, n)


def _splits(sizes, scales, slabs):
    starts = np.concatenate([[0], np.cumsum(sizes)[:-1]])
    return tuple((int(s), int(z), float(c), bool(b)) for s, z, c, b in zip(starts, sizes, scales, slabs))


def _mix_even(x2d, batch, g_pre, w_in, gla_w_gate_up, gla_b_gate, gla_norm):
    n = x2d.shape[0]
    ah = A_HEADS * HEAD_DIM
    dk2, dv2 = B_HEADS * B_DK, B_HEADS * B_DV
    sizes = (ah, ah, ah, dk2, dk2, dv2, dv2, LANES)
    scales = (HEAD_DIM ** -0.5 * LOG2E,) + (1.0,) * 7
    slabs = (True, True, True) + (False,) * 5
    qa, ka, va, qb, kb, vb, rb, gb = _norm_proj(x2d, g_pre, _layer0_w_in(w_in),
                                                _splits(sizes, scales, slabs),
                                                (F32, F32, F32, F32, F32, BF16, F32, F32),
                                                _row_tile(n, 1024))
    o_a = _band_attn(qa, ka, va, batch=batch,
                     patterns=tuple(sorted(((w // d, d) for w, d in A_PATTERNS), key=lambda p: -p[1])),
                     slopes=jnp.asarray(_alibi_slopes(A_HEADS) * np.float32(LOG2E)),
                     m0=jnp.full((A_HEADS,), NEG_INF, F32), l0=jnp.zeros((A_HEADS,), F32))
    w_gate = jnp.pad(gla_w_gate_up, ((0, LANES - B_GATE_RANK), (0, 0))).astype(BF16)
    o_b = _gla(qb, kb, vb, rb, gb, w_gate, gla_b_gate.reshape(1, dk2), gla_norm.reshape(1, B_DV), batch)
    return o_a, o_b


def _mix_odd(x2d, batch, g_pre, w_in, sinks, q_norm, w_uq, kv_norm, w_ukv):
    n = x2d.shape[0]
    seq = n // batch
    ch = C_HEADS * HEAD_DIM
    ckv = 2 * C_KV_HEADS * HEAD_DIM
    sizes = (ch, ckv, ckv, D_Q_RANK, D_KV_RANK, LANES)
    scales = (HEAD_DIM ** -0.5 * LOG2E,) + (1.0,) * 5
    slabs = (True, True, True, False, False, False)
    wq, wk, wv = _mla_weights(w_uq, w_ukv)
    qc, kc, vc, q, k, v = _norm_proj_mla(x2d, g_pre, _layer1_w_in(w_in), _splits(sizes, scales, slabs),
                                         BF16, q_norm.reshape(1, D_Q_RANK), kv_norm.reshape(1, D_KV_RANK),
                                         wq, wk, wv, _rope_tables(seq), seq, _row_tile(seq, 512))
    o_c = _band_attn(qc, kc, vc, batch=batch, patterns=((C_WINDOW - 1, 1),),
                     slopes=jnp.asarray(_alibi_slopes(C_HEADS) * np.float32(LOG2E)),
                     m0=sinks.astype(F32) * LOG2E, l0=jnp.ones((C_HEADS,), F32))
    o_d = _mla_attn(q, k, v, batch, _row_tile(seq // 2, 512))
    return o_c, o_d


def kernel(x, norm_mix_pre, norm_mix_post, norm_ffn_pre, norm_ffn_post, ffn_w_gate, ffn_w_up, ffn_w_down, ab_w_in, ab_w_out, gla_w_gate_up, gla_b_gate, gla_norm, cd_w_in, cd_w_out, swa_sinks, mla_q_norm, mla_w_uq, mla_kv_norm, mla_w_ukv):
    batch, seq, d = x.shape
    n = batch * seq
    x2d = x.reshape(n, d)
    depth = norm_mix_pre.shape[0]
    for layer in range(depth):
        i = layer // 2
        if layer % 2 == 0:
            o1, o2 = _mix_even(x2d, batch, norm_mix_pre[layer], ab_w_in[i], gla_w_gate_up[i],
                               gla_b_gate[i], gla_norm[i])
            w_out = ab_w_out[i].astype(BF16)
        else:
            o1, o2 = _mix_odd(x2d, batch, norm_mix_pre[layer], cd_w_in[i], swa_sinks[i],
                              mla_q_norm[i], mla_w_uq[i], mla_kv_norm[i], mla_w_ukv[i])
            w_out = cd_w_out[i].astype(BF16)
        x2d = _proj_ffn(o1, o2, w_out, x2d, norm_mix_post[layer],
                        norm_ffn_pre[layer], ffn_w_gate[layer].astype(BF16),
                        ffn_w_up[layer].astype(BF16), ffn_w_down[layer].astype(BF16),
                        norm_ffn_post[layer], _row_tile(n, 1024), 256)
    return x2d.reshape(batch, seq, d)
```
